```python
import math
import jax, jax.numpy as jnp
from jax import lax
import numpy as np

D_MODEL = 2048
BATCH = 2
SEQ = 4096
DEPTH = 4
DEC_BATCH = 1
DEC_SEQ = 8192
PAST_LEN = 128

N_MIXERS = 2
N_ATTN_LAYERS = (DEPTH + N_MIXERS - 1) // N_MIXERS
N_SSD_LAYERS = DEPTH // N_MIXERS

HEAD_DIM = 128
N_HEADS = D_MODEL // HEAD_DIM
N_KV_HEADS = N_HEADS // 4
Q_PER_KV = N_HEADS // N_KV_HEADS
WINDOW = 128
BLOCK = 128
REL_BUCKETS = 32
REL_MAX_DIST = 128
Q_DIM = N_HEADS * HEAD_DIM
KV_DIM = N_KV_HEADS * HEAD_DIM
QKV_DIM = Q_DIM + 2 * KV_DIM

SSD_EXPAND = 2
D_INNER = SSD_EXPAND * D_MODEL
SSD_HEAD_DIM = 64
SSD_HEADS = D_INNER // SSD_HEAD_DIM
SSD_GROUPS = 8
HEADS_PER_GROUP = SSD_HEADS // SSD_GROUPS
D_STATE = 128
CONV_WIDTH = 7
CONV_DIM = D_INNER + 2 * SSD_GROUPS * D_STATE
SSD_IN_DIM = D_INNER + CONV_DIM + 2 * SSD_HEADS
SSD_CHUNK = 128

N_EXPERTS = 16
EXPERT_FF = 2 * D_MODEL
CAPACITY_FACTOR = 2

NORM_EPS = 1e-6
GATED_NORM_EPS = 1e-5
NEG_INF = -1e30

kernel_name = "hybrid_bidir_swa_ssd_ec_encoder"


def _rmsnorm(x, w, eps=NORM_EPS):
    xf = x.astype(jnp.float32)
    xf = xf * lax.rsqrt(jnp.mean(xf * xf, axis=-1, keepdims=True) + eps)
    return xf.astype(x.dtype) * w


def _t5_buckets():
    qi = np.arange(BLOCK)[:, None]
    s = np.arange(3 * BLOCK)[None, :]
    rel = s - BLOCK - qi
    nb = REL_BUCKETS // 2
    ret = (rel > 0).astype(np.int32) * nb
    n = np.abs(rel)
    max_exact = nb // 2
    large = max_exact + (np.log(np.maximum(n, 1) / max_exact)
                         / np.log(REL_MAX_DIST / max_exact) * (nb - max_exact)).astype(np.int32)
    large = np.minimum(large, nb - 1)
    return ret + np.where(n < max_exact, n, large), rel


def _windowed_attention(x, w_qkv, w_o, sink, bias):
    b, L, _ = x.shape
    nb = L // BLOCK
    qkv = x @ w_qkv
    q = qkv[..., :Q_DIM].reshape(b, nb, BLOCK, N_KV_HEADS, Q_PER_KV, HEAD_DIM)
    k = qkv[..., Q_DIM:Q_DIM + KV_DIM].reshape(b, L, N_KV_HEADS, HEAD_DIM)
    v = qkv[..., Q_DIM + KV_DIM:].reshape(b, L, N_KV_HEADS, HEAD_DIM)
    pad = ((0, 0), (BLOCK, BLOCK), (0, 0), (0, 0))
    kp = jnp.pad(k, pad).reshape(b, nb + 2, BLOCK, N_KV_HEADS, HEAD_DIM)
    vp = jnp.pad(v, pad).reshape(b, nb + 2, BLOCK, N_KV_HEADS, HEAD_DIM)
    kb = jnp.concatenate([kp[:, :-2], kp[:, 1:-1], kp[:, 2:]], axis=2)
    vb = jnp.concatenate([vp[:, :-2], vp[:, 1:-1], vp[:, 2:]], axis=2)
    scores = jnp.einsum("bnqkgd,bnskd->bnkgqs", q, kb,
                        preferred_element_type=jnp.float32) * (HEAD_DIM ** -0.5)
    scores = scores + bias[None, None]
    _, rel = _t5_buckets()
    in_win = np.abs(rel) <= WINDOW
    kpos = (np.arange(nb)[:, None] - 1) * BLOCK + np.arange(3 * BLOCK)[None, :]
    in_seq = (kpos >= 0) & (kpos < L)
    mask = in_win[None] & in_seq[:, None, :]
    scores = jnp.where(mask[None, :, None, None], scores, NEG_INF)
    s = sink.astype(jnp.float32).reshape(N_KV_HEADS, Q_PER_KV)[None, None, :, :, None, None]
    m = jnp.maximum(jnp.max(scores, axis=-1, keepdims=True), s)
    p = jnp.exp(scores - m)
    probs = p / (jnp.sum(p, axis=-1, keepdims=True) + jnp.exp(s - m))
    out = jnp.einsum("bnkgqs,bnskd->bnqkgd", probs.astype(v.dtype), vb)
    return out.reshape(b, L, Q_DIM) @ w_o


def _depthwise_centred_conv(u, w, bias):
    c = u.shape[-1]
    half = CONV_WIDTH // 2
    out = lax.conv_general_dilated(u, w[:, None, :].astype(u.dtype), window_strides=(1,),
                                   padding=[(half, half)],
                                   dimension_numbers=("NWC", "WIO", "NWC"),
                                   feature_group_count=c)
    return out + bias


def _ssd_scan(xs, dt, A, Bm, Cm):
    b, L = xs.shape[:2]
    c = L // SSD_CHUNK
    dt = dt.reshape(b, c, SSD_CHUNK, SSD_GROUPS, HEADS_PER_GROUP)
    A = A.reshape(SSD_GROUPS, HEADS_PER_GROUP)
    X = xs.reshape(b, c, SSD_CHUNK, SSD_GROUPS, HEADS_PER_GROUP, SSD_HEAD_DIM) * dt[..., None]
    Bc = Bm.reshape(b, c, SSD_CHUNK, SSD_GROUPS, D_STATE)
    Cc = Cm.reshape(b, c, SSD_CHUNK, SSD_GROUPS, D_STATE)
    a_cs = jnp.cumsum(dt * A, axis=2)
    a_t = jnp.moveaxis(a_cs, 2, -1)
    causal = np.tril(np.ones((SSD_CHUNK, SSD_CHUNK), dtype=bool))
    decay = jnp.exp(jnp.where(causal, a_t[..., :, None] - a_t[..., None, :], -jnp.inf))
    cb = jnp.einsum("bclgn,bcsgn->bcgls", Cc, Bc, preferred_element_type=jnp.float32)
    y_diag = jnp.einsum("bcgrls,bcsgrp->bclgrp", cb[:, :, :, None] * decay, X)
    decay_to_end = jnp.exp(a_cs[:, :, -1:] - a_cs)
    chunk_states = jnp.einsum("bclgn,bclgr,bclgrp->bcgrpn", Bc, decay_to_end, X)
    chunk_decay = jnp.exp(a_cs[:, :, -1])

    def step(h, inp):
        st, d = inp
        return d[..., None, None] * h + st, h

    h0 = jnp.zeros((b, SSD_GROUPS, HEADS_PER_GROUP, SSD_HEAD_DIM, D_STATE), jnp.float32)
    _, h_in = lax.scan(step, h0, (jnp.moveaxis(chunk_states, 1, 0), jnp.moveaxis(chunk_decay, 1, 0)))
    h_in = jnp.moveaxis(h_in, 0, 1)
    y_off = jnp.einsum("bclgn,bcgrpn,bclgr->bclgrp", Cc, h_in, jnp.exp(a_cs))
    return (y_diag + y_off).reshape(b, L, SSD_GROUPS, HEADS_PER_GROUP, SSD_HEAD_DIM)


def _ssd_mixer(x, w_in, conv_w, conv_b, dt_bias, A_log, D_skip, norm_w, w_out):
    b, L, _ = x.shape
    proj = x @ w_in
    z = proj[..., :D_INNER]
    xbc = jax.nn.silu(_depthwise_centred_conv(proj[..., D_INNER:D_INNER + CONV_DIM], conv_w, conv_b))
    dt_raw = proj[..., D_INNER + CONV_DIM:].astype(jnp.float32).reshape(b, L, 2, SSD_HEADS)
    dt = jax.nn.softplus(dt_raw + dt_bias.astype(jnp.float32))
    A = -jnp.exp(A_log.astype(jnp.float32))
    xs = xbc[..., :D_INNER].reshape(b, L, SSD_GROUPS, HEADS_PER_GROUP, SSD_HEAD_DIM)
    gn = SSD_GROUPS * D_STATE
    Bm = xbc[..., D_INNER:D_INNER + gn].reshape(b, L, SSD_GROUPS, D_STATE)
    Cm = xbc[..., D_INNER + gn:].reshape(b, L, SSD_GROUPS, D_STATE)
    flip = lambda a: jnp.flip(a, axis=1)
    y_f = _ssd_scan(xs, dt[:, :, 0], A[0], Bm, Cm)
    y_b = flip(_ssd_scan(flip(xs), flip(dt[:, :, 1]), A[1], flip(Bm), flip(Cm)))
    d = D_skip.astype(jnp.float32).reshape(SSD_GROUPS, HEADS_PER_GROUP)[..., None]
    y = (y_f + y_b + d * xs).reshape(b, L, D_INNER)
    g = (y * jax.nn.silu(z.astype(jnp.float32))).reshape(b, L, SSD_GROUPS, D_INNER // SSD_GROUPS)
    g = g * lax.rsqrt(jnp.mean(g * g, axis=-1, keepdims=True) + GATED_NORM_EPS)
    y = g.reshape(b, L, D_INNER).astype(x.dtype) * norm_w
    return y @ w_out


def _expert_choice_ffn(x, w_router, w_gate, w_up, w_down):
    shape = x.shape
    t = x.reshape(-1, shape[-1])
    n = t.shape[0]
    cap = CAPACITY_FACTOR * n // N_EXPERTS
    aff = jax.nn.softmax(jnp.matmul(t, w_router, preferred_element_type=jnp.float32), axis=-1)
    gate, idx = lax.top_k(aff.T, cap)
    xe = t[idx]
    h = jax.nn.silu(jnp.einsum("ecd,edf->ecf", xe, w_gate)) * jnp.einsum("ecd,edf->ecf", xe, w_up)
    ye = jnp.einsum("ecf,efd->ecd", h, w_down) * gate[..., None].astype(x.dtype)
    out = jnp.zeros_like(t).at[idx.reshape(-1)].add(ye.reshape(-1, shape[-1]))
    return out.reshape(shape)


def _trunk(x, rel_bias, mixer_norm_w, ffn_norm_w, final_norm_w,
           attn_w_qkv, attn_w_o, attn_sink,
           ssd_w_in, ssd_conv_w, ssd_conv_b, ssd_dt_bias, ssd_A_log, ssd_D, ssd_norm_w, ssd_w_out,
           router_w, expert_w_gate, expert_w_up, expert_w_down):
    buckets, _ = _t5_buckets()
    bias = jnp.transpose(rel_bias.astype(jnp.float32)[buckets], (2, 0, 1))
    bias = bias.reshape(N_KV_HEADS, Q_PER_KV, BLOCK, 3 * BLOCK)
    for i in range(DEPTH):
        h = _rmsnorm(x, mixer_norm_w[i])
        j = i // N_MIXERS
        if i % N_MIXERS == 0:
            x = x + _windowed_attention(h, attn_w_qkv[j], attn_w_o[j], attn_sink[j], bias)
        else:
            x = x + _ssd_mixer(h, ssd_w_in[j], ssd_conv_w[j], ssd_conv_b[j], ssd_dt_bias[j],
                               ssd_A_log[j], ssd_D[j], ssd_norm_w[j], ssd_w_out[j])
        x = x + _expert_choice_ffn(_rmsnorm(x, ffn_norm_w[i]), router_w[i],
                                   expert_w_gate[i], expert_w_up[i], expert_w_down[i])
    return _rmsnorm(x, final_norm_w)


def setup_inputs(seed: int = 0) -> dict:
    key = jax.random.key(seed)
    ks = jax.random.split(key, 24)
    f32 = jnp.float32
    nrm = lambda k, shape, scale: jax.random.normal(k, shape, f32) * scale
    u_dt = jax.random.uniform(ks[13], (N_SSD_LAYERS, 2, SSD_HEADS), f32)
    dt0 = jnp.exp(u_dt * (math.log(0.1) - math.log(0.001)) + math.log(0.001))
    return {
        "x_prompt": nrm(ks[0], (BATCH, SEQ, D_MODEL), 1.0),
        "x_sample": nrm(ks[1], (DEC_BATCH, DEC_SEQ, D_MODEL), 1.0),
        "rel_bias": nrm(ks[2], (REL_BUCKETS, N_HEADS), 0.5),
        "mixer_norm_w": 1.0 + nrm(ks[3], (DEPTH, D_MODEL), 0.02),
        "ffn_norm_w": 1.0 + nrm(ks[4], (DEPTH, D_MODEL), 0.02),
        "final_norm_w": 1.0 + nrm(ks[5], (D_MODEL,), 0.02),
        "attn_w_qkv": nrm(ks[6], (N_ATTN_LAYERS, D_MODEL, QKV_DIM), D_MODEL ** -0.5),
        "attn_w_o": nrm(ks[7], (N_ATTN_LAYERS, Q_DIM, D_MODEL), Q_DIM ** -0.5),
        "attn_sink": nrm(ks[8], (N_ATTN_LAYERS, N_HEADS), 1.0),
        "ssd_w_in": nrm(ks[9], (N_SSD_LAYERS, D_MODEL, SSD_IN_DIM), D_MODEL ** -0.5),
        "ssd_conv_w": nrm(ks[10], (N_SSD_LAYERS, CONV_WIDTH, CONV_DIM), CONV_WIDTH ** -0.5),
        "ssd_conv_b": nrm(ks[11], (N_SSD_LAYERS, CONV_DIM), 0.02),
        "ssd_dt_bias": dt0 + jnp.log(-jnp.expm1(-dt0)),
        "ssd_A_log": jnp.log(jax.random.uniform(ks[14], (N_SSD_LAYERS, 2, SSD_HEADS), f32, 1.0, 16.0)),
        "ssd_D": 1.0 + nrm(ks[15], (N_SSD_LAYERS, SSD_HEADS), 0.1),
        "ssd_norm_w": 1.0 + nrm(ks[16], (N_SSD_LAYERS, D_INNER), 0.02),
        "ssd_w_out": nrm(ks[17], (N_SSD_LAYERS, D_INNER, D_MODEL), D_INNER ** -0.5),
        "router_w": nrm(ks[18], (DEPTH, D_MODEL, N_EXPERTS), D_MODEL ** -0.5),
        "expert_w_gate": nrm(ks[19], (DEPTH, N_EXPERTS, D_MODEL, EXPERT_FF), D_MODEL ** -0.5),
        "expert_w_up": nrm(ks[20], (DEPTH, N_EXPERTS, D_MODEL, EXPERT_FF), D_MODEL ** -0.5),
        "expert_w_down": nrm(ks[21], (DEPTH, N_EXPERTS, EXPERT_FF, D_MODEL), EXPERT_FF ** -0.5),
    }


def reference(x_prompt, x_sample, rel_bias, mixer_norm_w, ffn_norm_w, final_norm_w,
              attn_w_qkv, attn_w_o, attn_sink,
              ssd_w_in, ssd_conv_w, ssd_conv_b, ssd_dt_bias, ssd_A_log, ssd_D, ssd_norm_w, ssd_w_out,
              router_w, expert_w_gate, expert_w_up, expert_w_down):
    y_prompt = _trunk(x_prompt, rel_bias, mixer_norm_w, ffn_norm_w, final_norm_w,
                      attn_w_qkv, attn_w_o, attn_sink,
                      ssd_w_in, ssd_conv_w, ssd_conv_b, ssd_dt_bias, ssd_A_log, ssd_D, ssd_norm_w, ssd_w_out,
                      router_w, expert_w_gate, expert_w_up, expert_w_down)
    y_sample = _trunk(x_sample, rel_bias, mixer_norm_w, ffn_norm_w, final_norm_w,
                      attn_w_qkv, attn_w_o, attn_sink,
                      ssd_w_in, ssd_conv_w, ssd_conv_b, ssd_dt_bias, ssd_A_log, ssd_D, ssd_norm_w, ssd_w_out,
                      router_w, expert_w_gate, expert_w_up, expert_w_down)
    return (y_prompt, y_sample)
```

```python
import functools
import math

import jax
import jax.numpy as jnp
import numpy as np
from jax import lax
from jax.experimental import pallas as pl
from jax.experimental.pallas import tpu as pltpu

D_MODEL = 2048
BATCH = 2
SEQ = 4096
DEPTH = 4
DEC_BATCH = 1
DEC_SEQ = 8192

N_MIXERS = 2

HEAD_DIM = 128
WINDOW = 128
BLOCK = 128
REL_BUCKETS = 32
REL_MAX_DIST = 128
KV_GROUP = 4

SSD_EXPAND = 2
SSD_HEAD_DIM = 64
SSD_GROUPS = 8
D_STATE = 128
CONV_WIDTH = 7
SSD_CHUNK = 128

N_EXPERTS = 16
FF_MULT = 2
CAPACITY_FACTOR = 2

NORM_EPS = 1e-6
GATED_NORM_EPS = 1e-5
NEG_INF = -1e30

LANES = 128
SUBLANES = 8
VMEM_LIMIT = 56 * 1024 * 1024

F32 = jnp.float32
BF16 = jnp.bfloat16


def _seq_lens():
    return [SEQ] * BATCH + [DEC_SEQ] * DEC_BATCH


def _set_lens():
    return [BATCH * SEQ, DEC_BATCH * DEC_SEQ]


def _tile(n, cands=(512, 384, 256, 128)):
    return next(c for c in cands if n % c == 0)


def _params(sem):
    return pltpu.CompilerParams(dimension_semantics=sem, vmem_limit_bytes=VMEM_LIMIT)


def _norm_matmul_kernel(x_ref, nw_ref, w_ref, o_ref, hn_ref):
    @pl.when(pl.program_id(1) == 0)
    def _():
        x = x_ref[...]
        ms = jnp.mean(x * x, axis=-1, keepdims=True)
        hn_ref[...] = ((x * lax.rsqrt(ms + NORM_EPS)) * nw_ref[...]).astype(BF16)

    o_ref[...] = jnp.dot(hn_ref[...], w_ref[...].astype(BF16),
                         preferred_element_type=F32).astype(o_ref.dtype)


def _norm_matmul(x, nw, layer, w, wl, col0, ncols, tn, out_dtype, tm):
    t, d = x.shape
    cb0 = col0 // tn
    return pl.pallas_call(
        _norm_matmul_kernel,
        out_shape=jax.ShapeDtypeStruct((t, ncols), out_dtype),
        grid=(t // tm, ncols // tn),
        in_specs=[
            pl.BlockSpec((tm, d), lambda i, j: (i, 0)),
            pl.BlockSpec((None, 1, d), lambda i, j: (layer, 0, 0)),
            pl.BlockSpec((None, d, tn), lambda i, j: (wl, 0, cb0 + j)),
        ],
        out_specs=pl.BlockSpec((tm, tn), lambda i, j: (i, j)),
        scratch_shapes=[pltpu.VMEM((tm, d), BF16)],
        compiler_params=_params(("arbitrary", "arbitrary")),
        name="norm_matmul",
    )(x, nw.reshape(nw.shape[0], 1, d), w)


def _matmul_residual_kernel(a_ref, w_ref, x_ref, o_ref):
    o_ref[...] = x_ref[...] + jnp.dot(a_ref[...], w_ref[...].astype(BF16),
                                      preferred_element_type=F32)


def _matmul_residual(a, w, wl, x, tm, tn):
    t, k = a.shape
    d = x.shape[1]
    return pl.pallas_call(
        _matmul_residual_kernel,
        out_shape=jax.ShapeDtypeStruct((t, d), F32),
        grid=(t // tm, d // tn),
        in_specs=[
            pl.BlockSpec((tm, k), lambda i, j: (i, 0)),
            pl.BlockSpec((None, k, tn), lambda i, j: (wl, 0, j)),
            pl.BlockSpec((tm, tn), lambda i, j: (i, j)),
        ],
        out_specs=pl.BlockSpec((tm, tn), lambda i, j: (i, j)),
        compiler_params=_params(("arbitrary", "arbitrary")),
        name="matmul_residual",
    )(a, w, x)


def _t5_bias_table(rel_bias):
    n_heads = D_MODEL // HEAD_DIM
    qi = np.arange(BLOCK)[:, None]
    s = np.arange(3 * BLOCK)[None, :]
    rel = s - BLOCK - qi
    nb = REL_BUCKETS // 2
    ret = (rel > 0).astype(np.int32) * nb
    n = np.abs(rel)
    max_exact = nb // 2
    large = max_exact + (np.log(np.maximum(n, 1) / max_exact)
                         / np.log(REL_MAX_DIST / max_exact) * (nb - max_exact)).astype(np.int32)
    large = np.minimum(large, nb - 1)
    buckets = ret + np.where(n < max_exact, n, large)
    bias = jnp.transpose(rel_bias.astype(F32)[buckets], (2, 0, 1))
    bias = jnp.where(jnp.asarray(np.abs(rel) <= WINDOW)[None], bias, NEG_INF)
    return bias.reshape(n_heads // KV_GROUP, KV_GROUP, BLOCK, 3 * BLOCK)


def _attention_kernel(flags_ref, sink_ref, q_ref, kp_ref, kc_ref, kn_ref, vp_ref, vc_ref, vn_ref,
                      bias_ref, o_ref):
    kv = pl.program_id(0)
    n = pl.program_id(1)
    has_prev = flags_ref[2 * n] > 0
    has_next = flags_ref[2 * n + 1] > 0
    k = jnp.concatenate([kp_ref[...], kc_ref[...], kn_ref[...]], axis=0)
    v = jnp.concatenate([vp_ref[...], vc_ref[...], vn_ref[...]], axis=0)
    col = lax.broadcasted_iota(jnp.int32, (BLOCK, 3 * BLOCK), 1)
    in_seq = jnp.logical_and(jnp.logical_or(col >= BLOCK, has_prev),
                             jnp.logical_or(col < 2 * BLOCK, has_next))
    q = q_ref[...]
    scale = HEAD_DIM ** -0.5
    for g in range(KV_GROUP):
        qg = q[:, g * HEAD_DIM:(g + 1) * HEAD_DIM]
        s = lax.dot_general(qg, k, (((1,), (1,)), ((), ())), preferred_element_type=F32)
        s = s * scale + bias_ref[g]
        s = jnp.where(in_seq, s, NEG_INF)
        sink = sink_ref[kv * KV_GROUP + g]
        m = jnp.maximum(jnp.max(s, axis=-1, keepdims=True), sink)
        p = jnp.exp(s - m)
        den = jnp.sum(p, axis=-1, keepdims=True) + jnp.exp(sink - m)
        o = jnp.dot(p.astype(BF16), v, preferred_element_type=F32) / den
        o_ref[:, g * HEAD_DIM:(g + 1) * HEAD_DIM] = o.astype(o_ref.dtype)


def _attention(qkv, bias, sink):
    t = qkv.shape[0]
    n_heads = D_MODEL // HEAD_DIM
    n_kv = n_heads // KV_GROUP
    nblk = t // BLOCK
    flags = np.ones((nblk, 2), np.int32)
    start = 0
    for ln in _seq_lens():
        flags[start // BLOCK, 0] = 0
        flags[(start + ln) // BLOCK - 1, 1] = 0
        start += ln
    flags = jnp.asarray(flags.reshape(-1))
    kcol = n_heads
    vcol = n_heads + n_kv
    gw = KV_GROUP * HEAD_DIM

    def prev(i):
        return jnp.maximum(i - 1, 0)

    def nxt(i):
        return jnp.minimum(i + 1, nblk - 1)

    blk = (BLOCK, HEAD_DIM)
    return pl.pallas_call(
        _attention_kernel,
        out_shape=jax.ShapeDtypeStruct((t, n_heads * HEAD_DIM), BF16),
        grid_spec=pltpu.PrefetchScalarGridSpec(
            num_scalar_prefetch=1,
            grid=(n_kv, nblk),
            in_specs=[
                pl.BlockSpec(memory_space=pltpu.SMEM),
                pl.BlockSpec((BLOCK, gw), lambda kv, i, f: (i, kv)),
                pl.BlockSpec(blk, lambda kv, i, f: (prev(i), kcol + kv)),
                pl.BlockSpec(blk, lambda kv, i, f: (i, kcol + kv)),
                pl.BlockSpec(blk, lambda kv, i, f: (nxt(i), kcol + kv)),
                pl.BlockSpec(blk, lambda kv, i, f: (prev(i), vcol + kv)),
                pl.BlockSpec(blk, lambda kv, i, f: (i, vcol + kv)),
                pl.BlockSpec(blk, lambda kv, i, f: (nxt(i), vcol + kv)),
                pl.BlockSpec((None, KV_GROUP, BLOCK, 3 * BLOCK), lambda kv, i, f: (kv, 0, 0, 0)),
            ],
            out_specs=pl.BlockSpec((BLOCK, gw), lambda kv, i, f: (i, kv)),
        ),
        compiler_params=_params(("arbitrary", "arbitrary")),
        name="window_attention",
    )(flags, sink.astype(F32), qkv, qkv, qkv, qkv, qkv, qkv, qkv, bias)


BF16_ROWS = 16


def _conv_kernel(cur_ref, prev_ref, next_ref, w_ref, b_ref, o_ref, ext_ref, *, tm, starts, ends):
    row0 = pl.program_id(0) * tm
    is_first = functools.reduce(jnp.logical_or, [row0 == s for s in starts])
    is_last = functools.reduce(jnp.logical_or, [row0 + tm == e for e in ends])
    halo = SUBLANES
    prev = prev_ref[...].astype(F32)[BF16_ROWS - halo:, :]
    nxt = next_ref[...].astype(F32)[:halo, :]
    ext_ref[0:halo, :] = jnp.where(is_first, 0.0, prev)
    ext_ref[halo:halo + tm, :] = cur_ref[...].astype(F32)
    ext_ref[halo + tm:2 * halo + tm, :] = jnp.where(is_last, 0.0, nxt)
    half = CONV_WIDTH // 2
    w = w_ref[...]
    acc = jnp.broadcast_to(b_ref[...], o_ref.shape).astype(F32)
    for k in range(CONV_WIDTH):
        off = halo - half + k
        acc = acc + w[k:k + 1, :] * ext_ref[off:off + tm, :]
    o_ref[...] = jax.nn.silu(acc).astype(o_ref.dtype)


def _conv_silu(zxbc, conv_w, conv_b, layer, col0, ncols, tm, tc):
    t = zxbc.shape[0]
    starts, ends = [], []
    s = 0
    for ln in _seq_lens():
        starts.append(s)
        ends.append(s + ln)
        s += ln
    cb0 = col0 // tc
    hb = tm // BF16_ROWS
    last_hb = t // BF16_ROWS - 1
    kern = functools.partial(_conv_kernel, tm=tm, starts=tuple(starts), ends=tuple(ends))
    return pl.pallas_call(
        kern,
        out_shape=jax.ShapeDtypeStruct((t, ncols), BF16),
        grid=(t // tm, ncols // tc),
        in_specs=[
            pl.BlockSpec((tm, tc), lambda i, j: (i, cb0 + j)),
            pl.BlockSpec((BF16_ROWS, tc), lambda i, j: (jnp.maximum(i * hb - 1, 0), cb0 + j)),
            pl.BlockSpec((BF16_ROWS, tc), lambda i, j: (jnp.minimum((i + 1) * hb, last_hb), cb0 + j)),
            pl.BlockSpec((None, CONV_WIDTH, tc), lambda i, j: (layer, 0, j)),
            pl.BlockSpec((None, 1, tc), lambda i, j: (layer, 0, j)),
        ],
        out_specs=pl.BlockSpec((tm, tc), lambda i, j: (i, j)),
        scratch_shapes=[pltpu.VMEM((tm + 2 * SUBLANES, tc), F32)],
        compiler_params=_params(("arbitrary", "arbitrary")),
        name="conv_silu",
    )(zxbc, zxbc, zxbc, conv_w, conv_b.reshape(conv_b.shape[0], 1, conv_b.shape[1]))


def _split3(a):
    a1 = a.astype(BF16)
    r = a - a1.astype(F32)
    a2 = r.astype(BF16)
    a3 = (r - a2.astype(F32)).astype(BF16)
    return a1, a2, a3


def _nt(a, b):
    return lax.dot_general(a, b, (((1,), (1,)), ((), ())), preferred_element_type=F32)


def _ssd_chunk_setup(dt_ref, dtb_ref, a_ref, grow_ref, dtrow_ref, wst_ref, eoff_ref, cd_ref):
    c = SSD_CHUNK
    nh = dtb_ref.shape[1] // 2
    x = dt_ref[...] + dtb_ref[...]
    dt = jnp.maximum(x, 0.0) + jnp.log1p(jnp.exp(-jnp.abs(x)))
    dt_row = dt.T
    a_row = dt_row * a_ref[...]
    ii = lax.broadcasted_iota(jnp.int32, (c, c), 0)
    jj = lax.broadcasted_iota(jnp.int32, (c, c), 1)
    tl = (jj <= ii).astype(BF16)
    tu = (jj >= ii).astype(BF16)
    parts = _split3(a_row)
    p_row = sum(_nt(p, tl) for p in parts)
    r_row = sum(_nt(p, tu) for p in parts)
    hrow = lax.broadcasted_iota(jnp.int32, a_row.shape, 0)
    fwd = hrow < nh
    g_row = jnp.where(fwd, p_row, r_row)
    g_end = jnp.where(fwd, g_row[:, c - 1:c], g_row[:, 0:1])
    grow_ref[...] = g_row
    dtrow_ref[...] = dt_row
    wst_ref[...] = jnp.exp(g_end - g_row) * dt_row
    eoff_ref[...] = jnp.exp(g_row)
    cd_ref[...] = jnp.broadcast_to(jnp.exp(g_end), cd_ref.shape)


def _ssd_kernel(order_ref, reset_ref, dt_ref, dtb_ref, a_ref, x_ref, b_ref, c_ref, *rest, backward,
                hpg, nh):
    if backward:
        (o_ref, grow_ref, dtrow_ref, wst_ref, eoff_ref, cd_ref, s_ref) = rest
    else:
        (z_ref, yb_ref, dskip_ref, nw_ref, o_ref,
         grow_ref, dtrow_ref, wst_ref, eoff_ref, cd_ref, s_ref) = rest
    del order_ref
    i = pl.program_id(0)
    g = pl.program_id(1)
    c = SSD_CHUNK
    p = SSD_HEAD_DIM
    d = 1 if backward else 0

    @pl.when(g == 0)
    def _():
        _ssd_chunk_setup(dt_ref, dtb_ref, a_ref, grow_ref, dtrow_ref, wst_ref, eoff_ref, cd_ref)

    @pl.when(reset_ref[i] > 0)
    def _():
        s_ref[g] = jnp.zeros(s_ref.shape[1:], F32)

    x_t = x_ref[...].astype(F32).T
    bmat = b_ref[...]
    cmat = c_ref[...]
    gs = pl.multiple_of(g * hpg, hpg)

    def rows(ref, dd):
        return ref[pl.ds(dd * nh + gs, hpg), :]

    state = s_ref[g]
    eoff = rows(eoff_ref, d)
    wst = rows(wst_ref, d)
    cdr = rows(cd_ref, d)
    y_off = _nt(state.astype(BF16), cmat)
    xw, scaled_state, y_parts = [], [], []
    for r in range(hpg):
        sl = slice(r * p, (r + 1) * p)
        y_parts.append(y_off[sl] * eoff[r:r + 1, :])
        xw.append((x_t[sl] * wst[r:r + 1, :]).astype(BF16))
        scaled_state.append(state[sl] * cdr[r:r + 1, :])
    s_ref[g] = jnp.concatenate(scaled_state, axis=0) + jnp.dot(
        jnp.concatenate(xw, axis=0), bmat, preferred_element_type=F32)
    y_t = jnp.concatenate(y_parts, axis=0)

    if backward:
        o_ref[...] = y_t.T.astype(o_ref.dtype)
        return

    cbt = _nt(bmat, cmat)
    si = lax.broadcasted_iota(jnp.int32, (c, c), 0)
    li = lax.broadcasted_iota(jnp.int32, (c, c), 1)
    grow = (rows(grow_ref, 0), rows(grow_ref, 1))
    dtrow = (rows(dtrow_ref, 0), rows(dtrow_ref, 1))
    yd = []
    for r in range(hpg):
        sl = slice(r * p, (r + 1) * p)
        acc = None
        for dd in (0, 1):
            g_l = grow[dd][r:r + 1, :]
            g_s = jnp.broadcast_to(g_l, (c, c)).T
            causal = (si <= li) if dd == 0 else (si >= li)
            m_t = cbt * jnp.exp(jnp.where(causal, g_l - g_s, -jnp.inf))
            xd = (x_t[sl] * dtrow[dd][r:r + 1, :]).astype(BF16)
            term = jnp.dot(xd, m_t.astype(BF16), preferred_element_type=F32)
            acc = term if acc is None else acc + term
        yd.append(acc)
    y_t = y_t + jnp.concatenate(yd, axis=0) + dskip_ref[...] * x_t
    y = y_t.T + yb_ref[...].astype(F32)
    gated = y * jax.nn.silu(z_ref[...].astype(F32))
    gated = gated * lax.rsqrt(jnp.mean(gated * gated, axis=-1, keepdims=True) + GATED_NORM_EPS)
    o_ref[...] = (gated * nw_ref[...]).astype(o_ref.dtype)


def _ssd_scan(zxbc, xbc, dt_raw, dt_bias, a_log, d_skip, norm_w, layer):
    t = xbc.shape[0]
    d_inner = SSD_EXPAND * D_MODEL
    nh = d_inner // SSD_HEAD_DIM
    hpg = nh // SSD_GROUPS
    gw = hpg * SSD_HEAD_DIM
    c = SSD_CHUNK
    nchunks = t // c
    assert D_STATE == c

    fwd_order, bwd_order, reset = [], [], []
    s = 0
    for ln in _seq_lens():
        ids = list(range(s // c, (s + ln) // c))
        fwd_order += ids
        bwd_order += ids[::-1]
        reset += [1] + [0] * (len(ids) - 1)
        s += ln
    reset = jnp.asarray(np.asarray(reset, np.int32))

    dtb = dt_bias[layer].astype(F32).reshape(1, 2 * nh)
    a_neg = -jnp.exp(a_log[layer].astype(F32)).reshape(2 * nh, 1)
    a_rep = jnp.broadcast_to(a_neg, (2 * nh, c))
    dskip = jnp.broadcast_to(jnp.repeat(d_skip[layer].astype(F32), SSD_HEAD_DIM)[:, None], (d_inner, c))
    nw = norm_w[layer].reshape(1, d_inner)

    bcol0 = d_inner // D_STATE
    ccol0 = bcol0 + SSD_GROUPS
    tab = pltpu.VMEM((2 * nh, c), F32)

    def common_specs():
        return [
            pl.BlockSpec((c, 2 * nh), lambda i, g, o, r: (o[i], 0)),
            pl.BlockSpec((1, 2 * nh), lambda i, g, o, r: (0, 0)),
            pl.BlockSpec((2 * nh, c), lambda i, g, o, r: (0, 0)),
            pl.BlockSpec((c, gw), lambda i, g, o, r: (o[i], g)),
            pl.BlockSpec((c, D_STATE), lambda i, g, o, r: (o[i], bcol0 + g)),
            pl.BlockSpec((c, D_STATE), lambda i, g, o, r: (o[i], ccol0 + g)),
        ]

    y_bwd = pl.pallas_call(
        functools.partial(_ssd_kernel, backward=True, hpg=hpg, nh=nh),
        out_shape=jax.ShapeDtypeStruct((t, d_inner), BF16),
        grid_spec=pltpu.PrefetchScalarGridSpec(
            num_scalar_prefetch=2,
            grid=(nchunks, SSD_GROUPS),
            in_specs=common_specs(),
            out_specs=pl.BlockSpec((c, gw), lambda i, g, o, r: (o[i], g)),
            scratch_shapes=[tab, tab, tab, tab, tab, pltpu.VMEM((SSD_GROUPS, gw, D_STATE), F32)],
        ),
        compiler_params=_params(("arbitrary", "arbitrary")),
        name="ssd_backward",
    )(jnp.asarray(np.asarray(bwd_order, np.int32)), reset, dt_raw, dtb, a_rep, xbc, xbc, xbc)

    return pl.pallas_call(
        functools.partial(_ssd_kernel, backward=False, hpg=hpg, nh=nh),
        out_shape=jax.ShapeDtypeStruct((t, d_inner), BF16),
        grid_spec=pltpu.PrefetchScalarGridSpec(
            num_scalar_prefetch=2,
            grid=(nchunks, SSD_GROUPS),
            in_specs=common_specs() + [
                pl.BlockSpec((c, gw), lambda i, g, o, r: (o[i], g)),
                pl.BlockSpec((c, gw), lambda i, g, o, r: (o[i], g)),
                pl.BlockSpec((gw, c), lambda i, g, o, r: (g, 0)),
                pl.BlockSpec((1, gw), lambda i, g, o, r: (0, g)),
            ],
            out_specs=pl.BlockSpec((c, gw), lambda i, g, o, r: (o[i], g)),
            scratch_shapes=[tab, tab, tab, tab, tab, pltpu.VMEM((SSD_GROUPS, gw, D_STATE), F32)],
        ),
        compiler_params=_params(("arbitrary", "arbitrary")),
        name="ssd_forward",
    )(jnp.asarray(np.asarray(fwd_order, np.int32)), reset, dt_raw, dtb, a_rep, xbc, xbc, xbc,
      zxbc, y_bwd, dskip, nw)


def _router_kernel(x_ref, nw_ref, wr_ref, xa_ref, *, n_exp):
    x = x_ref[...]
    d = x.shape[1]
    ms = jnp.mean(x * x, axis=-1, keepdims=True)
    hn = (x * lax.rsqrt(ms + NORM_EPS)) * nw_ref[...]
    h1 = hn.astype(BF16)
    h2 = (hn - h1.astype(F32)).astype(BF16)
    w = wr_ref[...]
    w1 = w.astype(BF16)
    w2 = (w - w1.astype(F32)).astype(BF16)
    logits = (jnp.dot(h1, w1, preferred_element_type=F32) + jnp.dot(h1, w2, preferred_element_type=F32)
              + jnp.dot(h2, w1, preferred_element_type=F32))
    lane = lax.broadcasted_iota(jnp.int32, logits.shape, 1)
    logits = jnp.where(lane < n_exp, logits, -jnp.inf)
    m = jnp.max(logits, axis=-1, keepdims=True)
    e = jnp.exp(logits - m)
    xa_ref[:, :d] = hn
    xa_ref[:, d:] = e / jnp.sum(e, axis=-1, keepdims=True)


def _router(x, nw, wr, layer, tm):
    t, d = x.shape
    n_exp = wr.shape[-1]
    wr = jnp.pad(wr, ((0, 0), (0, 0), (0, LANES - n_exp)))
    return pl.pallas_call(
        functools.partial(_router_kernel, n_exp=n_exp),
        out_shape=jax.ShapeDtypeStruct((t, d + LANES), F32),
        grid=(t // tm,),
        in_specs=[
            pl.BlockSpec((tm, d), lambda i: (i, 0)),
            pl.BlockSpec((None, 1, d), lambda i: (layer, 0, 0)),
            pl.BlockSpec((None, d, LANES), lambda i: (layer, 0, 0)),
        ],
        out_specs=pl.BlockSpec((tm, d + LANES), lambda i: (i, 0)),
        compiler_params=_params(("arbitrary",)),
        name="router",
    )(x, nw.reshape(nw.shape[0], 1, d), wr)


def _route_kernel(aff_ref, idx_ref, bits_ref, sel_ref, cnt_ref, *, cap, n_tok, tok0, jb):
    nt, ne, _ = aff_ref.shape
    bits_ref[...] = pltpu.bitcast(aff_ref[...], jnp.int32)

    def count_ge(cand):
        def body(ti, acc):
            return acc + (bits_ref[ti] >= cand).astype(jnp.int32)
        acc = lax.fori_loop(0, nt, body, jnp.zeros((ne, LANES), jnp.int32))
        return jnp.sum(acc, axis=1, keepdims=True)

    def bisect(b, prefix):
        cand = prefix | lax.shift_left(jnp.int32(1), 30 - b)
        return jnp.where(count_ge(cand) >= cap, cand, prefix)

    thr = lax.fori_loop(0, 31, bisect, jnp.zeros((ne, 1), jnp.int32))

    def count_gt(ti, acc):
        return acc + (bits_ref[ti] > thr).astype(jnp.int32)
    n_gt = jnp.sum(lax.fori_loop(0, nt, count_gt, jnp.zeros((ne, LANES), jnp.int32)), axis=1, keepdims=True)
    need = (cap - n_gt).astype(F32)

    ii = lax.broadcasted_iota(jnp.int32, (LANES, LANES), 0)
    jj = lax.broadcasted_iota(jnp.int32, (LANES, LANES), 1)
    excl = (ii < jj).astype(BF16)

    def tie_body(ti, run):
        b = bits_ref[ti]
        eq = (b == thr)
        rank = run + jnp.dot(eq.astype(BF16), excl, preferred_element_type=F32)
        sel = jnp.logical_or(b > thr, jnp.logical_and(eq, rank < need))
        sel_ref[ti] = sel.astype(F32)
        return run + jnp.sum(eq.astype(F32), axis=1, keepdims=True)
    lax.fori_loop(0, nt, tie_body, jnp.zeros((ne, 1), F32))

    def cnt_body(ti, run):
        s = sel_ref[ti]
        inc = run + jnp.dot(s.astype(BF16), excl, preferred_element_type=F32) + s
        cnt_ref[ti] = inc
        return run + jnp.sum(s, axis=1, keepdims=True)
    lax.fori_loop(0, nt, cnt_body, jnp.zeros((ne, 1), F32))

    ones = jnp.ones((SUBLANES, LANES), BF16)
    for e in range(ne):
        for j0 in range(0, cap, jb):
            jcol = (lax.broadcasted_iota(jnp.int32, (jb, LANES), 0) + j0).astype(F32)

            def body(ti, acc, e=e, jcol=jcol):
                row = cnt_ref[ti][e:e + 1, :]
                return acc + (row <= jcol).astype(F32)
            acc = lax.fori_loop(0, nt, body, jnp.zeros((jb, LANES), F32))
            tok = _nt(ones, acc.astype(BF16))
            idx_ref[e:e + 1, j0:j0 + jb] = tok[0:1, :].astype(jnp.int32) + tok0


def _route(aff3, cap, tok0):
    nt, ne, _ = aff3.shape
    jb = min(cap, 256)
    return pl.pallas_call(
        functools.partial(_route_kernel, cap=cap, n_tok=nt * LANES, tok0=tok0, jb=jb),
        out_shape=jax.ShapeDtypeStruct((ne, cap), jnp.int32),
        scratch_shapes=[pltpu.VMEM((nt, ne, LANES), jnp.int32),
                        pltpu.VMEM((nt, ne, LANES), F32),
                        pltpu.VMEM((nt, ne, LANES), F32)],
        compiler_params=pltpu.CompilerParams(vmem_limit_bytes=VMEM_LIMIT),
        name="route_topk",
    )(aff3)


def _ffn_kernel(idx_ref, xa_hbm, wg_ref, wu_ref, wd_ref, o_ref, xg_ref, xb_ref, gate_ref, sem,
                *, cap, d, n_exp):
    s = pl.program_id(0)
    e = pl.program_id(1)
    f = pl.program_id(2)

    @pl.when(f == 0)
    def _():
        base = (s * n_exp + e) * cap

        def issue(j, carry):
            tok = idx_ref[base + j]
            pltpu.make_async_copy(xa_hbm.at[pl.ds(tok, 1)], xg_ref.at[pl.ds(j, 1)], sem).start()
            return carry
        lax.fori_loop(0, cap, issue, 0)
        pltpu.make_async_copy(xa_hbm.at[pl.ds(0, cap)], xg_ref, sem).wait()
        xb_ref[...] = xg_ref[:, :d].astype(BF16)
        aff = xg_ref[:, d:]
        lane = lax.broadcasted_iota(jnp.int32, aff.shape, 1)
        gate_ref[...] = jnp.sum(jnp.where(lane == e, aff, 0.0), axis=1, keepdims=True)

    x = xb_ref[...]
    gt = jnp.dot(x, wg_ref[...].astype(BF16), preferred_element_type=F32)
    up = jnp.dot(x, wu_ref[...].astype(BF16), preferred_element_type=F32)
    h = (jax.nn.silu(gt) * up).astype(BF16)
    y = jnp.dot(h, wd_ref[...].astype(BF16), preferred_element_type=F32)

    @pl.when(f == 0)
    def _():
        o_ref[...] = y

    @pl.when(f > 0)
    def _():
        o_ref[...] += y

    @pl.when(f == pl.num_programs(2) - 1)
    def _():
        o_ref[...] = o_ref[...] * gate_ref[...]


def _expert_ffn(idx, xa, wg, wu, wd, layer, cap, tf):
    n_sets = len(_set_lens())
    d = D_MODEL
    n_exp = N_EXPERTS
    ff = wg.shape[-1]
    return pl.pallas_call(
        functools.partial(_ffn_kernel, cap=cap, d=d, n_exp=n_exp),
        out_shape=jax.ShapeDtypeStruct((n_sets, n_exp, cap, d), F32),
        grid_spec=pltpu.PrefetchScalarGridSpec(
            num_scalar_prefetch=1,
            grid=(n_sets, n_exp, ff // tf),
            in_specs=[
                pl.BlockSpec(memory_space=pl.ANY),
                pl.BlockSpec((None, None, d, tf), lambda s, e, f, idx: (layer, e, 0, f)),
                pl.BlockSpec((None, None, d, tf), lambda s, e, f, idx: (layer, e, 0, f)),
                pl.BlockSpec((None, None, tf, d), lambda s, e, f, idx: (layer, e, f, 0)),
            ],
            out_specs=pl.BlockSpec((None, None, cap, d), lambda s, e, f, idx: (s, e, 0, 0)),
            scratch_shapes=[pltpu.VMEM((cap, d + LANES), F32), pltpu.VMEM((cap, d), BF16),
                            pltpu.VMEM((cap, 1), F32), pltpu.SemaphoreType.DMA],
        ),
        compiler_params=_params(("arbitrary", "arbitrary", "arbitrary")),
        name="expert_ffn",
    )(idx, xa, wg, wu, wd)


def _combine_kernel(idx_ref, ye_ref, x_in, x_hbm, buf_ref, sem_in, sem_out, *, cap, n_exp):
    del x_in
    s = pl.program_id(0)
    e = pl.program_id(1)
    base = (s * n_exp + e) * cap

    def gather(j, carry):
        tok = idx_ref[base + j]
        pltpu.make_async_copy(x_hbm.at[pl.ds(tok, 1)], buf_ref.at[pl.ds(j, 1)], sem_in).start()
        return carry
    lax.fori_loop(0, cap, gather, 0)
    pltpu.make_async_copy(x_hbm.at[pl.ds(0, cap)], buf_ref, sem_in).wait()
    buf_ref[...] = buf_ref[...] + ye_ref[...]

    def scatter(j, carry):
        tok = idx_ref[base + j]
        pltpu.make_async_copy(buf_ref.at[pl.ds(j, 1)], x_hbm.at[pl.ds(tok, 1)], sem_out).start()
        return carry
    lax.fori_loop(0, cap, scatter, 0)
    pltpu.make_async_copy(buf_ref, x_hbm.at[pl.ds(0, cap)], sem_out).wait()


def _combine(idx, ye, x, cap):
    n_sets, n_exp = ye.shape[:2]
    d = x.shape[1]
    return pl.pallas_call(
        functools.partial(_combine_kernel, cap=cap, n_exp=n_exp),
        out_shape=jax.ShapeDtypeStruct(x.shape, x.dtype),
        grid_spec=pltpu.PrefetchScalarGridSpec(
            num_scalar_prefetch=1,
            grid=(n_sets, n_exp),
            in_specs=[
                pl.BlockSpec((None, None, cap, d), lambda s, e, idx: (s, e, 0, 0)),
                pl.BlockSpec(memory_space=pl.ANY),
            ],
            out_specs=pl.BlockSpec(memory_space=pl.ANY),
            scratch_shapes=[pltpu.VMEM((cap, d), F32), pltpu.SemaphoreType.DMA, pltpu.SemaphoreType.DMA],
        ),
        input_output_aliases={2: 0},
        compiler_params=_params(("arbitrary", "arbitrary")),
        name="expert_combine",
    )(idx, ye, x)


def _expert_choice_ffn(x, ffn_norm_w, router_w, wg, wu, wd, layer):
    t = x.shape[0]
    xa = _router(x, ffn_norm_w, router_w, layer, tm=min(512, t))
    aff = xa[:, x.shape[1]:x.shape[1] + N_EXPERTS]
    idx = []
    tok0 = 0
    cap = None
    for n in _set_lens():
        assert cap is None or cap == CAPACITY_FACTOR * n // N_EXPERTS
        cap = CAPACITY_FACTOR * n // N_EXPERTS
        a3 = aff[tok0:tok0 + n].reshape(n // LANES, LANES, N_EXPERTS).transpose(0, 2, 1)
        idx.append(_route(a3, cap, tok0))
        tok0 += n
    idx = jnp.stack(idx).reshape(-1)
    ye = _expert_ffn(idx, xa, wg, wu, wd, layer, cap, tf=min(256, wg.shape[-1]))
    return _combine(idx, ye, x, cap)


def _final_norm_kernel(x_ref, nw_ref, o_ref):
    x = x_ref[...]
    ms = jnp.mean(x * x, axis=-1, keepdims=True)
    o_ref[...] = (x * lax.rsqrt(ms + NORM_EPS)) * nw_ref[...]


def _final_norm(x, nw, row0, nrows, tm):
    d = x.shape[1]
    rb0 = row0 // tm
    return pl.pallas_call(
        _final_norm_kernel,
        out_shape=jax.ShapeDtypeStruct((nrows, d), F32),
        grid=(nrows // tm,),
        in_specs=[pl.BlockSpec((tm, d), lambda i: (rb0 + i, 0)),
                  pl.BlockSpec((1, d), lambda i: (0, 0))],
        out_specs=pl.BlockSpec((tm, d), lambda i: (i, 0)),
        compiler_params=_params(("arbitrary",)),
        name="final_norm",
    )(x, nw.reshape(1, d))


def kernel(x_prompt, x_sample, rel_bias, mixer_norm_w, ffn_norm_w, final_norm_w, attn_w_qkv, attn_w_o, attn_sink, ssd_w_in, ssd_conv_w, ssd_conv_b, ssd_dt_bias, ssd_A_log, ssd_D, ssd_norm_w, ssd_w_out, router_w, expert_w_gate, expert_w_up, expert_w_down):
    d = D_MODEL
    x = jnp.concatenate([x_prompt.reshape(-1, d), x_sample.reshape(-1, d)], axis=0)
    t = x.shape[0]
    n_heads = d // HEAD_DIM
    q_dim = n_heads * HEAD_DIM
    qkv_dim = q_dim + 2 * (n_heads // KV_GROUP) * HEAD_DIM
    d_inner = SSD_EXPAND * d
    nh = d_inner // SSD_HEAD_DIM
    conv_dim = d_inner + 2 * SSD_GROUPS * D_STATE
    tm = min(1024, t)
    bias = _t5_bias_table(rel_bias)

    for i in range(DEPTH):
        j = i // N_MIXERS
        if i % N_MIXERS == 0:
            qkv = _norm_matmul(x, mixer_norm_w, i, attn_w_qkv, j, 0, qkv_dim, _tile(qkv_dim), BF16, tm)
            a = _attention(qkv, bias, attn_sink[j])
            x = _matmul_residual(a, attn_w_o, j, x, tm, _tile(d))
        else:
            zx_dim = d_inner + conv_dim
            zxbc = _norm_matmul(x, mixer_norm_w, i, ssd_w_in, j, 0, zx_dim, _tile(zx_dim), BF16, tm)
            dt_raw = _norm_matmul(x, mixer_norm_w, i, ssd_w_in, j, zx_dim, 2 * nh, 2 * nh, F32, tm)
            xbc = _conv_silu(zxbc, ssd_conv_w, ssd_conv_b, j, d_inner, conv_dim, min(512, SEQ),
                             _tile(math.gcd(d_inner, conv_dim)))
            y = _ssd_scan(zxbc, xbc, dt_raw, ssd_dt_bias, ssd_A_log, ssd_D, ssd_norm_w, j)
            x = _matmul_residual(y, ssd_w_out, j, x, tm, _tile(d))
        x = _expert_choice_ffn(x, ffn_norm_w, router_w, expert_w_gate, expert_w_up, expert_w_down, i)

    n_prompt = BATCH * SEQ
    n_sample = DEC_BATCH * DEC_SEQ
    y_prompt = _final_norm(x, final_norm_w, 0, n_prompt, min(1024, n_prompt))
    y_sample = _final_norm(x, final_norm_w, n_prompt, n_sample, min(1024, n_prompt))
    return (y_prompt.reshape(BATCH, SEQ, d), y_sample.reshape(DEC_BATCH, DEC_SEQ, d))
```

```python
import functools
import math

import jax
import jax.numpy as jnp
import numpy as np
from jax import lax
from jax.experimental import pallas as pl
from jax.experimental.pallas import tpu as pltpu

D_MODEL = 2048
BATCH = 2
SEQ = 4096
DEPTH = 4
DEC_BATCH = 1
DEC_SEQ = 8192

N_MIXERS = 2

HEAD_DIM = 128
WINDOW = 128
BLOCK = 128
REL_BUCKETS = 32
REL_MAX_DIST = 128
KV_GROUP = 4

SSD_EXPAND = 2
SSD_HEAD_DIM = 64
SSD_GROUPS = 8
D_STATE = 128
CONV_WIDTH = 7
SSD_CHUNK = 128

N_EXPERTS = 16
FF_MULT = 2
CAPACITY_FACTOR = 2

NORM_EPS = 1e-6
GATED_NORM_EPS = 1e-5
NEG_INF = -1e30

LANES = 128
SUBLANES = 8
VMEM_LIMIT = 56 * 1024 * 1024
DMA_ISSUE_UNROLL = 8

F32 = jnp.float32
BF16 = jnp.bfloat16


def _seq_lens():
    return [SEQ] * BATCH + [DEC_SEQ] * DEC_BATCH


def _set_lens():
    return [BATCH * SEQ, DEC_BATCH * DEC_SEQ]


WIDE_TILES = (1024, 768, 512, 384, 256, 128)


def _tile(n, cands=(512, 384, 256, 128)):
    return next(c for c in cands if n % c == 0)


def _params(sem):
    return pltpu.CompilerParams(dimension_semantics=sem, vmem_limit_bytes=VMEM_LIMIT)


def _rmsnorm_kernel(x_ref, nw_ref, o_ref):
    x = x_ref[...]
    ms = jnp.mean(x * x, axis=-1, keepdims=True)
    o_ref[...] = ((x * lax.rsqrt(ms + NORM_EPS)) * nw_ref[...]).astype(o_ref.dtype)


def _rmsnorm(x, nw, layer, tm):
    t, d = x.shape
    return pl.pallas_call(
        _rmsnorm_kernel,
        out_shape=jax.ShapeDtypeStruct((t, d), BF16),
        grid=(t // tm,),
        in_specs=[pl.BlockSpec((tm, d), lambda i: (i, 0)),
                  pl.BlockSpec((None, 1, d), lambda i: (layer, 0, 0))],
        out_specs=pl.BlockSpec((tm, d), lambda i: (i, 0)),
        compiler_params=_params(("arbitrary",)),
        name="rmsnorm",
    )(x, nw.reshape(nw.shape[0], 1, d))


def _matmul_kernel(a_ref, w_ref, *rest, residual):
    if residual:
        x_ref, o_ref, wb_ref = rest
    else:
        o_ref, wb_ref = rest

    @pl.when(pl.program_id(1) == 0)
    def _():
        wb_ref[...] = w_ref[...].astype(BF16)

    y = jnp.dot(a_ref[...], wb_ref[...], preferred_element_type=F32)
    if residual:
        y = x_ref[...] + y
    o_ref[...] = y.astype(o_ref.dtype)


def _matmul(a, w, wl, col0, ncols, tn, out_dtype, tm, x=None):
    t, k = a.shape
    cb0 = col0 // tn
    in_specs = [
        pl.BlockSpec((tm, k), lambda j, i: (i, 0)),
        pl.BlockSpec((None, k, tn), lambda j, i: (wl, 0, cb0 + j)),
    ]
    args = [a, w]
    if x is not None:
        in_specs.append(pl.BlockSpec((tm, tn), lambda j, i: (i, j)))
        args.append(x)
    return pl.pallas_call(
        functools.partial(_matmul_kernel, residual=x is not None),
        out_shape=jax.ShapeDtypeStruct((t, ncols), out_dtype),
        grid=(ncols // tn, t // tm),
        in_specs=in_specs,
        out_specs=pl.BlockSpec((tm, tn), lambda j, i: (i, j)),
        scratch_shapes=[pltpu.VMEM((k, tn), BF16)],
        compiler_params=_params(("arbitrary", "arbitrary")),
        name="matmul_residual" if x is not None else "matmul",
    )(*args)


def _t5_bias_table(rel_bias):
    n_heads = D_MODEL // HEAD_DIM
    qi = np.arange(BLOCK)[:, None]
    s = np.arange(3 * BLOCK)[None, :]
    rel = s - BLOCK - qi
    nb = REL_BUCKETS // 2
    ret = (rel > 0).astype(np.int32) * nb
    n = np.abs(rel)
    max_exact = nb // 2
    large = max_exact + (np.log(np.maximum(n, 1) / max_exact)
                         / np.log(REL_MAX_DIST / max_exact) * (nb - max_exact)).astype(np.int32)
    large = np.minimum(large, nb - 1)
    buckets = ret + np.where(n < max_exact, n, large)
    bias = jnp.transpose(rel_bias.astype(F32)[buckets], (2, 0, 1))
    bias = jnp.where(jnp.asarray(np.abs(rel) <= WINDOW)[None], bias, NEG_INF)
    return bias.reshape(n_heads // KV_GROUP, KV_GROUP, BLOCK, 3 * BLOCK)


def _attention_kernel(flags_ref, sink_ref, q_ref, kp_ref, kc_ref, kn_ref, vp_ref, vc_ref, vn_ref,
                      bias_ref, o_ref):
    kv = pl.program_id(0)
    n = pl.program_id(1)
    has_prev = flags_ref[2 * n] > 0
    has_next = flags_ref[2 * n + 1] > 0
    k = jnp.concatenate([kp_ref[...], kc_ref[...], kn_ref[...]], axis=0)
    v = jnp.concatenate([vp_ref[...], vc_ref[...], vn_ref[...]], axis=0)
    col = lax.broadcasted_iota(jnp.int32, (BLOCK, 3 * BLOCK), 1)
    in_seq = jnp.logical_and(jnp.logical_or(col >= BLOCK, has_prev),
                             jnp.logical_or(col < 2 * BLOCK, has_next))
    q = q_ref[...]
    qs = jnp.concatenate([q[:, g * HEAD_DIM:(g + 1) * HEAD_DIM] for g in range(KV_GROUP)], axis=0)
    rows = KV_GROUP * BLOCK
    s = lax.dot_general(qs, k, (((1,), (1,)), ((), ())), preferred_element_type=F32)
    s = s * (HEAD_DIM ** -0.5) + bias_ref[...].reshape(rows, 3 * BLOCK)
    s = jnp.where(jnp.concatenate([in_seq] * KV_GROUP, axis=0), s, NEG_INF)
    head = lax.broadcasted_iota(jnp.int32, (rows, 1), 0) // BLOCK
    sink = jnp.zeros((rows, 1), F32)
    for g in range(KV_GROUP):
        sink = jnp.where(head == g, sink_ref[kv * KV_GROUP + g], sink)
    m = jnp.maximum(jnp.max(s, axis=-1, keepdims=True), sink)
    p = jnp.exp(s - m)
    den = jnp.sum(p, axis=-1, keepdims=True) + jnp.exp(sink - m)
    o = jnp.dot(p.astype(BF16), v, preferred_element_type=F32) / den
    for g in range(KV_GROUP):
        o_ref[:, g * HEAD_DIM:(g + 1) * HEAD_DIM] = o[g * BLOCK:(g + 1) * BLOCK].astype(o_ref.dtype)


def _attention(qkv, bias, sink):
    t = qkv.shape[0]
    n_heads = D_MODEL // HEAD_DIM
    n_kv = n_heads // KV_GROUP
    nblk = t // BLOCK
    flags = np.ones((nblk, 2), np.int32)
    start = 0
    for ln in _seq_lens():
        flags[start // BLOCK, 0] = 0
        flags[(start + ln) // BLOCK - 1, 1] = 0
        start += ln
    flags = jnp.asarray(flags.reshape(-1))
    kcol = n_heads
    vcol = n_heads + n_kv
    gw = KV_GROUP * HEAD_DIM

    def prev(i):
        return jnp.maximum(i - 1, 0)

    def nxt(i):
        return jnp.minimum(i + 1, nblk - 1)

    blk = (BLOCK, HEAD_DIM)
    return pl.pallas_call(
        _attention_kernel,
        out_shape=jax.ShapeDtypeStruct((t, n_heads * HEAD_DIM), BF16),
        grid_spec=pltpu.PrefetchScalarGridSpec(
            num_scalar_prefetch=1,
            grid=(n_kv, nblk),
            in_specs=[
                pl.BlockSpec(memory_space=pltpu.SMEM),
                pl.BlockSpec((BLOCK, gw), lambda kv, i, f: (i, kv)),
                pl.BlockSpec(blk, lambda kv, i, f: (prev(i), kcol + kv)),
                pl.BlockSpec(blk, lambda kv, i, f: (i, kcol + kv)),
                pl.BlockSpec(blk, lambda kv, i, f: (nxt(i), kcol + kv)),
                pl.BlockSpec(blk, lambda kv, i, f: (prev(i), vcol + kv)),
                pl.BlockSpec(blk, lambda kv, i, f: (i, vcol + kv)),
                pl.BlockSpec(blk, lambda kv, i, f: (nxt(i), vcol + kv)),
                pl.BlockSpec((None, KV_GROUP, BLOCK, 3 * BLOCK), lambda kv, i, f: (kv, 0, 0, 0)),
            ],
            out_specs=pl.BlockSpec((BLOCK, gw), lambda kv, i, f: (i, kv)),
        ),
        compiler_params=_params(("arbitrary", "arbitrary")),
        name="window_attention",
    )(flags, sink.astype(F32), qkv, qkv, qkv, qkv, qkv, qkv, qkv, bias)


BF16_ROWS = 16


def _conv_kernel(cur_ref, prev_ref, next_ref, w_ref, b_ref, o_ref, ext_ref, *, tm, starts, ends):
    row0 = pl.program_id(0) * tm
    is_first = functools.reduce(jnp.logical_or, [row0 == s for s in starts])
    is_last = functools.reduce(jnp.logical_or, [row0 + tm == e for e in ends])
    halo = SUBLANES
    prev = prev_ref[...].astype(F32)[BF16_ROWS - halo:, :]
    nxt = next_ref[...].astype(F32)[:halo, :]
    ext_ref[0:halo, :] = jnp.where(is_first, 0.0, prev)
    ext_ref[halo:halo + tm, :] = cur_ref[...].astype(F32)
    ext_ref[halo + tm:2 * halo + tm, :] = jnp.where(is_last, 0.0, nxt)
    half = CONV_WIDTH // 2
    w = w_ref[...]
    acc = jnp.broadcast_to(b_ref[...], o_ref.shape).astype(F32)
    for k in range(CONV_WIDTH):
        off = halo - half + k
        acc = acc + w[k:k + 1, :] * ext_ref[off:off + tm, :]
    o_ref[...] = jax.nn.silu(acc).astype(o_ref.dtype)


def _conv_silu(zxbc, conv_w, conv_b, layer, col0, ncols, tm, tc):
    t = zxbc.shape[0]
    starts, ends = [], []
    s = 0
    for ln in _seq_lens():
        starts.append(s)
        ends.append(s + ln)
        s += ln
    cb0 = col0 // tc
    hb = tm // BF16_ROWS
    last_hb = t // BF16_ROWS - 1
    kern = functools.partial(_conv_kernel, tm=tm, starts=tuple(starts), ends=tuple(ends))
    return pl.pallas_call(
        kern,
        out_shape=jax.ShapeDtypeStruct((t, ncols), BF16),
        grid=(t // tm, ncols // tc),
        in_specs=[
            pl.BlockSpec((tm, tc), lambda i, j: (i, cb0 + j)),
            pl.BlockSpec((BF16_ROWS, tc), lambda i, j: (jnp.maximum(i * hb - 1, 0), cb0 + j)),
            pl.BlockSpec((BF16_ROWS, tc), lambda i, j: (jnp.minimum((i + 1) * hb, last_hb), cb0 + j)),
            pl.BlockSpec((None, CONV_WIDTH, tc), lambda i, j: (layer, 0, j)),
            pl.BlockSpec((None, 1, tc), lambda i, j: (layer, 0, j)),
        ],
        out_specs=pl.BlockSpec((tm, tc), lambda i, j: (i, j)),
        scratch_shapes=[pltpu.VMEM((tm + 2 * SUBLANES, tc), F32)],
        compiler_params=_params(("arbitrary", "arbitrary")),
        name="conv_silu",
    )(zxbc, zxbc, zxbc, conv_w, conv_b.reshape(conv_b.shape[0], 1, conv_b.shape[1]))


def _split3(a):
    a1 = a.astype(BF16)
    r = a - a1.astype(F32)
    a2 = r.astype(BF16)
    a3 = (r - a2.astype(F32)).astype(BF16)
    return a1, a2, a3


def _nt(a, b):
    return lax.dot_general(a, b, (((1,), (1,)), ((), ())), preferred_element_type=F32)


def _ssd_chunk_setup(dt_ref, dtb_ref, a_ref, grow_ref, dtrow_ref, wst_ref, eoff_ref, cd_ref):
    c = SSD_CHUNK
    nh = dtb_ref.shape[1] // 2
    x = dt_ref[...] + dtb_ref[...]
    dt = jnp.maximum(x, 0.0) + jnp.log1p(jnp.exp(-jnp.abs(x)))
    dt_row = dt.T
    a_row = dt_row * a_ref[...]
    ii = lax.broadcasted_iota(jnp.int32, (c, c), 0)
    jj = lax.broadcasted_iota(jnp.int32, (c, c), 1)
    tl = (jj <= ii).astype(BF16)
    tu = (jj >= ii).astype(BF16)
    parts = _split3(a_row)
    p_row = sum(_nt(p, tl) for p in parts)
    r_row = sum(_nt(p, tu) for p in parts)
    hrow = lax.broadcasted_iota(jnp.int32, a_row.shape, 0)
    fwd = hrow < nh
    g_row = jnp.where(fwd, p_row, r_row)
    g_end = jnp.where(fwd, g_row[:, c - 1:c], g_row[:, 0:1])
    grow_ref[...] = g_row
    dtrow_ref[...] = dt_row
    wst_ref[...] = jnp.exp(g_end - g_row) * dt_row
    eoff_ref[...] = jnp.exp(g_row)
    cd_ref[...] = jnp.broadcast_to(jnp.exp(g_end), cd_ref.shape)


def _ssd_kernel(order_ref, reset_ref, dt_ref, dtb_ref, a_ref, x_ref, b_ref, c_ref, *rest, backward,
                hpg, nh):
    if backward:
        (o_ref, grow_ref, dtrow_ref, wst_ref, eoff_ref, cd_ref, s_ref) = rest
    else:
        (z_ref, yb_ref, dskip_ref, nw_ref, o_ref,
         grow_ref, dtrow_ref, wst_ref, eoff_ref, cd_ref, s_ref) = rest
    del order_ref
    c = SSD_CHUNK
    p = SSD_HEAD_DIM
    gw = hpg * p
    d = 1 if backward else 0

    _ssd_chunk_setup(dt_ref, dtb_ref, a_ref, grow_ref, dtrow_ref, wst_ref, eoff_ref, cd_ref)

    @pl.when(reset_ref[pl.program_id(0)] > 0)
    def _():
        s_ref[...] = jnp.zeros(s_ref.shape, F32)

    si = lax.broadcasted_iota(jnp.int32, (c, c), 0)
    li = lax.broadcasted_iota(jnp.int32, (c, c), 1)

    for g in range(SSD_GROUPS):
        cols = slice(g * gw, (g + 1) * gw)
        ncols = slice(g * D_STATE, (g + 1) * D_STATE)

        def rows(ref, dd, g=g):
            r0 = dd * nh + g * hpg
            return ref[r0:r0 + hpg, :]

        x_t = x_ref[:, cols].astype(F32).T
        bmat = b_ref[:, ncols]
        cmat = c_ref[:, ncols]

        state = s_ref[g]
        eoff = rows(eoff_ref, d)
        wst = rows(wst_ref, d)
        cdr = rows(cd_ref, d)
        y_off = _nt(state.astype(BF16), cmat)
        xw, scaled_state, y_parts = [], [], []
        for r in range(hpg):
            sl = slice(r * p, (r + 1) * p)
            y_parts.append(y_off[sl] * eoff[r:r + 1, :])
            xw.append((x_t[sl] * wst[r:r + 1, :]).astype(BF16))
            scaled_state.append(state[sl] * cdr[r:r + 1, :])
        s_ref[g] = jnp.concatenate(scaled_state, axis=0) + jnp.dot(
            jnp.concatenate(xw, axis=0), bmat, preferred_element_type=F32)
        y_t = jnp.concatenate(y_parts, axis=0)

        if backward:
            o_ref[:, cols] = y_t.T.astype(o_ref.dtype)
            continue

        cbt = _nt(bmat, cmat)
        grow = (rows(grow_ref, 0), rows(grow_ref, 1))
        dtrow = (rows(dtrow_ref, 0), rows(dtrow_ref, 1))
        yd = []
        for r in range(hpg):
            sl = slice(r * p, (r + 1) * p)
            acc = None
            for dd in (0, 1):
                g_l = grow[dd][r:r + 1, :]
                g_s = jnp.broadcast_to(g_l, (c, c)).T
                causal = (si <= li) if dd == 0 else (si >= li)
                m_t = cbt * jnp.exp(jnp.where(causal, g_l - g_s, -jnp.inf))
                xd = (x_t[sl] * dtrow[dd][r:r + 1, :]).astype(BF16)
                term = jnp.dot(xd, m_t.astype(BF16), preferred_element_type=F32)
                acc = term if acc is None else acc + term
            yd.append(acc)
        y_t = y_t + jnp.concatenate(yd, axis=0) + dskip_ref[cols, :] * x_t
        y = y_t.T + yb_ref[:, cols].astype(F32)
        gated = y * jax.nn.silu(z_ref[:, cols].astype(F32))
        gated = gated * lax.rsqrt(jnp.mean(gated * gated, axis=-1, keepdims=True) + GATED_NORM_EPS)
        o_ref[:, cols] = (gated * nw_ref[:, cols]).astype(o_ref.dtype)


def _ssd_scan(zxbc, xbc, dt_raw, dt_bias, a_log, d_skip, norm_w, layer):
    t = xbc.shape[0]
    d_inner = SSD_EXPAND * D_MODEL
    nh = d_inner // SSD_HEAD_DIM
    hpg = nh // SSD_GROUPS
    gw = hpg * SSD_HEAD_DIM
    c = SSD_CHUNK
    nchunks = t // c
    gn = SSD_GROUPS * D_STATE
    assert D_STATE == c and d_inner % gn == 0

    fwd_order, bwd_order, reset = [], [], []
    s = 0
    for ln in _seq_lens():
        ids = list(range(s // c, (s + ln) // c))
        fwd_order += ids
        bwd_order += ids[::-1]
        reset += [1] + [0] * (len(ids) - 1)
        s += ln
    reset = jnp.asarray(np.asarray(reset, np.int32))

    dtb = dt_bias[layer].astype(F32).reshape(1, 2 * nh)
    a_neg = -jnp.exp(a_log[layer].astype(F32)).reshape(2 * nh, 1)
    a_rep = jnp.broadcast_to(a_neg, (2 * nh, c))
    dskip = jnp.broadcast_to(jnp.repeat(d_skip[layer].astype(F32), SSD_HEAD_DIM)[:, None], (d_inner, c))
    nw = norm_w[layer].reshape(1, d_inner)

    bblk = d_inner // gn
    tab = pltpu.VMEM((2 * nh, c), F32)
    state = pltpu.VMEM((SSD_GROUPS, gw, D_STATE), F32)

    def common_specs():
        return [
            pl.BlockSpec((c, 2 * nh), lambda i, o, r: (o[i], 0)),
            pl.BlockSpec((1, 2 * nh), lambda i, o, r: (0, 0)),
            pl.BlockSpec((2 * nh, c), lambda i, o, r: (0, 0)),
            pl.BlockSpec((c, d_inner), lambda i, o, r: (o[i], 0)),
            pl.BlockSpec((c, gn), lambda i, o, r: (o[i], bblk)),
            pl.BlockSpec((c, gn), lambda i, o, r: (o[i], bblk + 1)),
        ]

    y_bwd = pl.pallas_call(
        functools.partial(_ssd_kernel, backward=True, hpg=hpg, nh=nh),
        out_shape=jax.ShapeDtypeStruct((t, d_inner), BF16),
        grid_spec=pltpu.PrefetchScalarGridSpec(
            num_scalar_prefetch=2,
            grid=(nchunks,),
            in_specs=common_specs(),
            out_specs=pl.BlockSpec((c, d_inner), lambda i, o, r: (o[i], 0)),
            scratch_shapes=[tab, tab, tab, tab, tab, state],
        ),
        compiler_params=_params(("arbitrary",)),
        name="ssd_backward",
    )(jnp.asarray(np.asarray(bwd_order, np.int32)), reset, dt_raw, dtb, a_rep, xbc, xbc, xbc)

    return pl.pallas_call(
        functools.partial(_ssd_kernel, backward=False, hpg=hpg, nh=nh),
        out_shape=jax.ShapeDtypeStruct((t, d_inner), BF16),
        grid_spec=pltpu.PrefetchScalarGridSpec(
            num_scalar_prefetch=2,
            grid=(nchunks,),
            in_specs=common_specs() + [
                pl.BlockSpec((c, d_inner), lambda i, o, r: (o[i], 0)),
                pl.BlockSpec((c, d_inner), lambda i, o, r: (o[i], 0)),
                pl.BlockSpec((d_inner, c), lambda i, o, r: (0, 0)),
                pl.BlockSpec((1, d_inner), lambda i, o, r: (0, 0)),
            ],
            out_specs=pl.BlockSpec((c, d_inner), lambda i, o, r: (o[i], 0)),
            scratch_shapes=[tab, tab, tab, tab, tab, state],
        ),
        compiler_params=_params(("arbitrary",)),
        name="ssd_forward",
    )(jnp.asarray(np.asarray(fwd_order, np.int32)), reset, dt_raw, dtb, a_rep, xbc, xbc, xbc,
      zxbc, y_bwd, dskip, nw)


def _router_kernel(x_ref, nw_ref, wr_ref, xa_ref, *, n_exp):
    x = x_ref[...]
    d = x.shape[1]
    ms = jnp.mean(x * x, axis=-1, keepdims=True)
    hn = (x * lax.rsqrt(ms + NORM_EPS)) * nw_ref[...]
    h1 = hn.astype(BF16)
    h2 = (hn - h1.astype(F32)).astype(BF16)
    w = wr_ref[...]
    w1 = w.astype(BF16)
    w2 = (w - w1.astype(F32)).astype(BF16)
    logits = (jnp.dot(h1, w1, preferred_element_type=F32) + jnp.dot(h1, w2, preferred_element_type=F32)
              + jnp.dot(h2, w1, preferred_element_type=F32))
    lane = lax.broadcasted_iota(jnp.int32, logits.shape, 1)
    logits = jnp.where(lane < n_exp, logits, -jnp.inf)
    m = jnp.max(logits, axis=-1, keepdims=True)
    e = jnp.exp(logits - m)
    xa_ref[:, :d] = hn
    xa_ref[:, d:] = e / jnp.sum(e, axis=-1, keepdims=True)


def _router(x, nw, wr, layer, tm):
    t, d = x.shape
    n_exp = wr.shape[-1]
    wr = jnp.pad(wr, ((0, 0), (0, 0), (0, LANES - n_exp)))
    return pl.pallas_call(
        functools.partial(_router_kernel, n_exp=n_exp),
        out_shape=jax.ShapeDtypeStruct((t, d + LANES), F32),
        grid=(t // tm,),
        in_specs=[
            pl.BlockSpec((tm, d), lambda i: (i, 0)),
            pl.BlockSpec((None, 1, d), lambda i: (layer, 0, 0)),
            pl.BlockSpec((None, d, LANES), lambda i: (layer, 0, 0)),
        ],
        out_specs=pl.BlockSpec((tm, d + LANES), lambda i: (i, 0)),
        compiler_params=_params(("arbitrary",)),
        name="router",
    )(x, nw.reshape(nw.shape[0], 1, d), wr)


def _route_kernel(aff_ref, idx_ref, bits_ref, sel_ref, cnt_ref, *, cap, n_tok, tok0, jb):
    nt, ne, _ = aff_ref.shape
    bits_ref[...] = pltpu.bitcast(aff_ref[...], jnp.int32)

    def count_ge(cand):
        def body(ti, acc):
            return acc + (bits_ref[ti] >= cand).astype(jnp.int32)
        acc = lax.fori_loop(0, nt, body, jnp.zeros((ne, LANES), jnp.int32))
        return jnp.sum(acc, axis=1, keepdims=True)

    def bisect(b, prefix):
        cand = prefix | lax.shift_left(jnp.int32(1), 30 - b)
        return jnp.where(count_ge(cand) >= cap, cand, prefix)

    thr = lax.fori_loop(0, 31, bisect, jnp.zeros((ne, 1), jnp.int32))

    def count_gt(ti, acc):
        return acc + (bits_ref[ti] > thr).astype(jnp.int32)
    n_gt = jnp.sum(lax.fori_loop(0, nt, count_gt, jnp.zeros((ne, LANES), jnp.int32)), axis=1, keepdims=True)
    need = (cap - n_gt).astype(F32)

    ii = lax.broadcasted_iota(jnp.int32, (LANES, LANES), 0)
    jj = lax.broadcasted_iota(jnp.int32, (LANES, LANES), 1)
    excl = (ii < jj).astype(BF16)

    def tie_body(ti, run):
        b = bits_ref[ti]
        eq = (b == thr)
        rank = run + jnp.dot(eq.astype(BF16), excl, preferred_element_type=F32)
        sel = jnp.logical_or(b > thr, jnp.logical_and(eq, rank < need))
        sel_ref[ti] = sel.astype(F32)
        return run + jnp.sum(eq.astype(F32), axis=1, keepdims=True)
    lax.fori_loop(0, nt, tie_body, jnp.zeros((ne, 1), F32))

    def cnt_body(ti, run):
        s = sel_ref[ti]
        inc = run + jnp.dot(s.astype(BF16), excl, preferred_element_type=F32) + s
        cnt_ref[ti] = inc
        return run + jnp.sum(s, axis=1, keepdims=True)
    lax.fori_loop(0, nt, cnt_body, jnp.zeros((ne, 1), F32))

    ones = jnp.ones((SUBLANES, LANES), BF16)
    for e in range(ne):
        for j0 in range(0, cap, jb):
            jcol = (lax.broadcasted_iota(jnp.int32, (jb, LANES), 0) + j0).astype(F32)

            def body(ti, acc, e=e, jcol=jcol):
                row = cnt_ref[ti][e:e + 1, :]
                return acc + (row <= jcol).astype(F32)
            acc = lax.fori_loop(0, nt, body, jnp.zeros((jb, LANES), F32))
            tok = _nt(ones, acc.astype(BF16))
            idx_ref[e:e + 1, j0:j0 + jb] = tok[0:1, :].astype(jnp.int32) + tok0


def _route(aff3, cap, tok0):
    nt, ne, _ = aff3.shape
    jb = min(cap, 256)
    return pl.pallas_call(
        functools.partial(_route_kernel, cap=cap, n_tok=nt * LANES, tok0=tok0, jb=jb),
        out_shape=jax.ShapeDtypeStruct((ne, cap), jnp.int32),
        scratch_shapes=[pltpu.VMEM((nt, ne, LANES), jnp.int32),
                        pltpu.VMEM((nt, ne, LANES), F32),
                        pltpu.VMEM((nt, ne, LANES), F32)],
        compiler_params=pltpu.CompilerParams(vmem_limit_bytes=VMEM_LIMIT),
        name="route_topk",
    )(aff3)


def _gather_rows(idx_ref, base, n, src_hbm, dst_ref, sem):
    def issue(j, carry):
        tok = idx_ref[base + j]
        pltpu.make_async_copy(src_hbm.at[pl.ds(tok, 1)], dst_ref.at[pl.ds(j, 1)], sem).start()
        return carry
    lax.fori_loop(0, n, issue, 0, unroll=DMA_ISSUE_UNROLL)
    pltpu.make_async_copy(src_hbm.at[pl.ds(0, n)], dst_ref, sem).wait()


def _ffn_kernel(idx_ref, xa_hbm, wg_ref, wu_ref, wd_ref, o_ref, xg_ref, xb_ref, gate_ref, h_ref, sem,
                *, cap, d, n_exp, nf):
    s = pl.program_id(0)
    e = pl.program_id(1)
    st = pl.program_id(2)

    @pl.when(st == 0)
    def _():
        _gather_rows(idx_ref, (s * n_exp + e) * cap, cap, xa_hbm, xg_ref, sem)
        xb_ref[...] = xg_ref[:, :d].astype(BF16)
        aff = xg_ref[:, d:]
        lane = lax.broadcasted_iota(jnp.int32, aff.shape, 1)
        gate_ref[...] = jnp.sum(jnp.where(lane == e, aff, 0.0), axis=1, keepdims=True)

    @pl.when(st < nf)
    def _():
        tf = wg_ref.shape[1]
        w = jnp.concatenate([wg_ref[...].astype(BF16), wu_ref[...].astype(BF16)], axis=1)
        gu = jnp.dot(xb_ref[...], w, preferred_element_type=F32)
        h_ref[st] = (jax.nn.silu(gu[:, :tf]) * gu[:, tf:]).astype(BF16)

    @pl.when(st >= nf)
    def _():
        h = jnp.concatenate([h_ref[f] for f in range(nf)], axis=1)
        y = jnp.dot(h, wd_ref[...].astype(BF16), preferred_element_type=F32)
        o_ref[...] = y * gate_ref[...]


def _expert_ffn(idx, xa, wg, wu, wd, layer, cap, tf, tn):
    n_sets = len(_set_lens())
    d = D_MODEL
    n_exp = N_EXPERTS
    ff = wg.shape[-1]
    nf = ff // tf

    def f_of(st):
        return jnp.minimum(st, nf - 1)

    def n_of(st):
        return jnp.maximum(st - nf, 0)

    return pl.pallas_call(
        functools.partial(_ffn_kernel, cap=cap, d=d, n_exp=n_exp, nf=nf),
        out_shape=jax.ShapeDtypeStruct((n_sets, n_exp, cap, d), F32),
        grid_spec=pltpu.PrefetchScalarGridSpec(
            num_scalar_prefetch=1,
            grid=(n_sets, n_exp, nf + d // tn),
            in_specs=[
                pl.BlockSpec(memory_space=pl.ANY),
                pl.BlockSpec((None, None, d, tf), lambda s, e, st, idx: (layer, e, 0, f_of(st))),
                pl.BlockSpec((None, None, d, tf), lambda s, e, st, idx: (layer, e, 0, f_of(st))),
                pl.BlockSpec((None, None, ff, tn), lambda s, e, st, idx: (layer, e, 0, n_of(st))),
            ],
            out_specs=pl.BlockSpec((None, None, cap, tn), lambda s, e, st, idx: (s, e, 0, n_of(st))),
            scratch_shapes=[pltpu.VMEM((cap, d + LANES), F32), pltpu.VMEM((cap, d), BF16),
                            pltpu.VMEM((cap, 1), F32), pltpu.VMEM((nf, cap, tf), BF16),
                            pltpu.SemaphoreType.DMA],
        ),
        compiler_params=_params(("arbitrary", "arbitrary", "arbitrary")),
        name="expert_ffn",
    )(idx, xa, wg, wu, wd)


def _combine_kernel(idx_ref, ye_ref, x_in, x_hbm, buf_ref, sem_in, sem_out, *, cap, n_exp):
    del x_in
    s = pl.program_id(0)
    e = pl.program_id(1)
    base = (s * n_exp + e) * cap

    def gather(j, carry):
        tok = idx_ref[base + j]
        pltpu.make_async_copy(x_hbm.at[pl.ds(tok, 1)], buf_ref.at[pl.ds(j, 1)], sem_in).start()
        return carry
    lax.fori_loop(0, cap, gather, 0, unroll=DMA_ISSUE_UNROLL)
    pltpu.make_async_copy(x_hbm.at[pl.ds(0, cap)], buf_ref, sem_in).wait()
    buf_ref[...] = buf_ref[...] + ye_ref[...]

    def scatter(j, carry):
        tok = idx_ref[base + j]
        pltpu.make_async_copy(buf_ref.at[pl.ds(j, 1)], x_hbm.at[pl.ds(tok, 1)], sem_out).start()
        return carry
    lax.fori_loop(0, cap, scatter, 0, unroll=DMA_ISSUE_UNROLL)
    pltpu.make_async_copy(buf_ref, x_hbm.at[pl.ds(0, cap)], sem_out).wait()


def _combine(idx, ye, x, cap):
    n_sets, n_exp = ye.shape[:2]
    d = x.shape[1]
    return pl.pallas_call(
        functools.partial(_combine_kernel, cap=cap, n_exp=n_exp),
        out_shape=jax.ShapeDtypeStruct(x.shape, x.dtype),
        grid_spec=pltpu.PrefetchScalarGridSpec(
            num_scalar_prefetch=1,
            grid=(n_sets, n_exp),
            in_specs=[
                pl.BlockSpec((None, None, cap, d), lambda s, e, idx: (s, e, 0, 0)),
                pl.BlockSpec(memory_space=pl.ANY),
            ],
            out_specs=pl.BlockSpec(memory_space=pl.ANY),
            scratch_shapes=[pltpu.VMEM((cap, d), F32), pltpu.SemaphoreType.DMA, pltpu.SemaphoreType.DMA],
        ),
        input_output_aliases={2: 0},
        compiler_params=_params(("arbitrary", "arbitrary")),
        name="expert_combine",
    )(idx, ye, x)


def _expert_choice_ffn(x, ffn_norm_w, router_w, wg, wu, wd, layer):
    t = x.shape[0]
    xa = _router(x, ffn_norm_w, router_w, layer, tm=min(512, t))
    aff = xa[:, x.shape[1]:x.shape[1] + N_EXPERTS]
    idx = []
    tok0 = 0
    cap = None
    for n in _set_lens():
        assert cap is None or cap == CAPACITY_FACTOR * n // N_EXPERTS
        cap = CAPACITY_FACTOR * n // N_EXPERTS
        a3 = aff[tok0:tok0 + n].reshape(n // LANES, LANES, N_EXPERTS).transpose(0, 2, 1)
        idx.append(_route(a3, cap, tok0))
        tok0 += n
    idx = jnp.stack(idx).reshape(-1)
    ye = _expert_ffn(idx, xa, wg, wu, wd, layer, cap, tf=_tile(wg.shape[-1], (256, 128)),
                     tn=_tile(x.shape[1], (256, 128)))
    return _combine(idx, ye, x, cap)


def _final_norm_kernel(x_ref, nw_ref, o_ref):
    x = x_ref[...]
    ms = jnp.mean(x * x, axis=-1, keepdims=True)
    o_ref[...] = (x * lax.rsqrt(ms + NORM_EPS)) * nw_ref[...]


def _final_norm(x, nw, row0, nrows, tm):
    d = x.shape[1]
    rb0 = row0 // tm
    return pl.pallas_call(
        _final_norm_kernel,
        out_shape=jax.ShapeDtypeStruct((nrows, d), F32),
        grid=(nrows // tm,),
        in_specs=[pl.BlockSpec((tm, d), lambda i: (rb0 + i, 0)),
                  pl.BlockSpec((1, d), lambda i: (0, 0))],
        out_specs=pl.BlockSpec((tm, d), lambda i: (i, 0)),
        compiler_params=_params(("arbitrary",)),
        name="final_norm",
    )(x, nw.reshape(1, d))


def kernel(x_prompt, x_sample, rel_bias, mixer_norm_w, ffn_norm_w, final_norm_w, attn_w_qkv, attn_w_o, attn_sink, ssd_w_in, ssd_conv_w, ssd_conv_b, ssd_dt_bias, ssd_A_log, ssd_D, ssd_norm_w, ssd_w_out, router_w, expert_w_gate, expert_w_up, expert_w_down):
    d = D_MODEL
    x = jnp.concatenate([x_prompt.reshape(-1, d), x_sample.reshape(-1, d)], axis=0)
    t = x.shape[0]
    n_heads = d // HEAD_DIM
    q_dim = n_heads * HEAD_DIM
    qkv_dim = q_dim + 2 * (n_heads // KV_GROUP) * HEAD_DIM
    d_inner = SSD_EXPAND * d
    nh = d_inner // SSD_HEAD_DIM
    conv_dim = d_inner + 2 * SSD_GROUPS * D_STATE
    tm = min(1024, t)
    bias = _t5_bias_table(rel_bias)

    for i in range(DEPTH):
        j = i // N_MIXERS
        hn = _rmsnorm(x, mixer_norm_w, i, tm)
        if i % N_MIXERS == 0:
            qkv = _matmul(hn, attn_w_qkv, j, 0, qkv_dim, _tile(qkv_dim, WIDE_TILES), BF16, tm)
            a = _attention(qkv, bias, attn_sink[j])
            x = _matmul(a, attn_w_o, j, 0, d, _tile(d), F32, tm, x=x)
        else:
            zx_dim = d_inner + conv_dim
            zxbc = _matmul(hn, ssd_w_in, j, 0, zx_dim, _tile(zx_dim, WIDE_TILES), BF16, tm)
            dt_raw = _matmul(hn, ssd_w_in, j, zx_dim, 2 * nh, 2 * nh, F32, tm)
            xbc = _conv_silu(zxbc, ssd_conv_w, ssd_conv_b, j, d_inner, conv_dim, min(512, SEQ),
                             _tile(math.gcd(d_inner, conv_dim)))
            y = _ssd_scan(zxbc, xbc, dt_raw, ssd_dt_bias, ssd_A_log, ssd_D, ssd_norm_w, j)
            x = _matmul(y, ssd_w_out, j, 0, d, _tile(d), F32, tm, x=x)
        x = _expert_choice_ffn(x, ffn_norm_w, router_w, expert_w_gate, expert_w_up, expert_w_down, i)

    n_prompt = BATCH * SEQ
    n_sample = DEC_BATCH * DEC_SEQ
    y_prompt = _final_norm(x, final_norm_w, 0, n_prompt, min(1024, n_prompt))
    y_sample = _final_norm(x, final_norm_w, n_prompt, n_sample, min(1024, n_prompt))
    return (y_prompt.reshape(BATCH, SEQ, d), y_sample.reshape(DEC_BATCH, DEC_SEQ, d))
```

```python
import functools
import math

import jax
import jax.numpy as jnp
import numpy as np
from jax import lax
from jax.experimental import pallas as pl
from jax.experimental.pallas import tpu as pltpu

D_MODEL = 2048
BATCH = 2
SEQ = 4096
DEPTH = 4
DEC_BATCH = 1
DEC_SEQ = 8192

N_MIXERS = 2

HEAD_DIM = 128
WINDOW = 128
BLOCK = 128
REL_BUCKETS = 32
REL_MAX_DIST = 128
KV_GROUP = 4

SSD_EXPAND = 2
SSD_HEAD_DIM = 64
SSD_GROUPS = 8
D_STATE = 128
CONV_WIDTH = 7
SSD_CHUNK = 128

N_EXPERTS = 16
FF_MULT = 2
CAPACITY_FACTOR = 2

NORM_EPS = 1e-6
GATED_NORM_EPS = 1e-5
NEG_INF = -1e30

LANES = 128
SUBLANES = 8
VMEM_LIMIT = 60 * 1024 * 1024
DMA_ISSUE_UNROLL = 8

F32 = jnp.float32
BF16 = jnp.bfloat16


def _seq_lens():
    return [SEQ] * BATCH + [DEC_SEQ] * DEC_BATCH


def _set_lens():
    return [BATCH * SEQ, DEC_BATCH * DEC_SEQ]


WIDE_TILES = (1024, 768, 512, 384, 256, 128)


def _tile(n, cands=(512, 384, 256, 128)):
    return next(c for c in cands if n % c == 0)


def _params(sem):
    return pltpu.CompilerParams(dimension_semantics=sem, vmem_limit_bytes=VMEM_LIMIT)


def _rmsnorm_kernel(x_ref, nw_ref, o_ref):
    x = x_ref[...]
    ms = jnp.mean(x * x, axis=-1, keepdims=True)
    o_ref[...] = ((x * lax.rsqrt(ms + NORM_EPS)) * nw_ref[...]).astype(o_ref.dtype)


def _rmsnorm(x, nw, layer, tm):
    t, d = x.shape
    return pl.pallas_call(
        _rmsnorm_kernel,
        out_shape=jax.ShapeDtypeStruct((t, d), BF16),
        grid=(t // tm,),
        in_specs=[pl.BlockSpec((tm, d), lambda i: (i, 0)),
                  pl.BlockSpec((None, 1, d), lambda i: (layer, 0, 0))],
        out_specs=pl.BlockSpec((tm, d), lambda i: (i, 0)),
        compiler_params=_params(("arbitrary",)),
        name="rmsnorm",
    )(x, nw.reshape(nw.shape[0], 1, d))


def _matmul_kernel(a_ref, w_ref, *rest, residual):
    if residual:
        x_ref, o_ref, wb_ref = rest
    else:
        o_ref, wb_ref = rest

    @pl.when(pl.program_id(1) == 0)
    def _():
        wb_ref[...] = w_ref[...].astype(BF16)

    y = jnp.dot(a_ref[...], wb_ref[...], preferred_element_type=F32)
    if residual:
        y = x_ref[...] + y
    o_ref[...] = y.astype(o_ref.dtype)


def _matmul(a, w, wl, col0, ncols, tn, out_dtype, tm, x=None):
    t, k = a.shape
    cb0 = col0 // tn
    in_specs = [
        pl.BlockSpec((tm, k), lambda j, i: (i, 0)),
        pl.BlockSpec((None, k, tn), lambda j, i: (wl, 0, cb0 + j)),
    ]
    args = [a, w]
    if x is not None:
        in_specs.append(pl.BlockSpec((tm, tn), lambda j, i: (i, j)))
        args.append(x)
    return pl.pallas_call(
        functools.partial(_matmul_kernel, residual=x is not None),
        out_shape=jax.ShapeDtypeStruct((t, ncols), out_dtype),
        grid=(ncols // tn, t // tm),
        in_specs=in_specs,
        out_specs=pl.BlockSpec((tm, tn), lambda j, i: (i, j)),
        scratch_shapes=[pltpu.VMEM((k, tn), BF16)],
        compiler_params=_params(("arbitrary", "arbitrary")),
        name="matmul_residual" if x is not None else "matmul",
    )(*args)


def _t5_bias_table(rel_bias):
    n_heads = D_MODEL // HEAD_DIM
    qi = np.arange(BLOCK)[:, None]
    s = np.arange(3 * BLOCK)[None, :]
    rel = s - BLOCK - qi
    nb = REL_BUCKETS // 2
    ret = (rel > 0).astype(np.int32) * nb
    n = np.abs(rel)
    max_exact = nb // 2
    large = max_exact + (np.log(np.maximum(n, 1) / max_exact)
                         / np.log(REL_MAX_DIST / max_exact) * (nb - max_exact)).astype(np.int32)
    large = np.minimum(large, nb - 1)
    buckets = ret + np.where(n < max_exact, n, large)
    onehot = jnp.asarray(np.eye(REL_BUCKETS, dtype=np.float32)[buckets.reshape(-1)])
    bias = jnp.einsum("pb,bh->hp", onehot, rel_bias.astype(F32), precision=lax.Precision.HIGHEST)
    bias = bias.reshape(n_heads, BLOCK, 3 * BLOCK)
    bias = jnp.where(jnp.asarray(np.abs(rel) <= WINDOW)[None], bias, NEG_INF)
    return bias.reshape(n_heads // KV_GROUP, KV_GROUP, BLOCK, 3 * BLOCK)


def _attention_kernel(flags_ref, sink_ref, q_ref, kp_ref, kc_ref, kn_ref, vp_ref, vc_ref, vn_ref,
                      bias_ref, o_ref):
    kv = pl.program_id(0)
    n = pl.program_id(1)
    has_prev = flags_ref[2 * n] > 0
    has_next = flags_ref[2 * n + 1] > 0
    k = jnp.concatenate([kp_ref[...], kc_ref[...], kn_ref[...]], axis=0)
    v = jnp.concatenate([vp_ref[...], vc_ref[...], vn_ref[...]], axis=0)
    col = lax.broadcasted_iota(jnp.int32, (BLOCK, 3 * BLOCK), 1)
    in_seq = jnp.logical_and(jnp.logical_or(col >= BLOCK, has_prev),
                             jnp.logical_or(col < 2 * BLOCK, has_next))
    q = q_ref[...]
    qs = jnp.concatenate([q[:, g * HEAD_DIM:(g + 1) * HEAD_DIM] for g in range(KV_GROUP)], axis=0)
    rows = KV_GROUP * BLOCK
    s = lax.dot_general(qs, k, (((1,), (1,)), ((), ())), preferred_element_type=F32)
    s = s * (HEAD_DIM ** -0.5) + bias_ref[...].reshape(rows, 3 * BLOCK)
    s = jnp.where(jnp.concatenate([in_seq] * KV_GROUP, axis=0), s, NEG_INF)
    head = lax.broadcasted_iota(jnp.int32, (rows, 1), 0) // BLOCK
    sink = jnp.zeros((rows, 1), F32)
    for g in range(KV_GROUP):
        sink = jnp.where(head == g, sink_ref[kv * KV_GROUP + g], sink)
    m = jnp.maximum(jnp.max(s, axis=-1, keepdims=True), sink)
    p = jnp.exp(s - m)
    den = jnp.sum(p, axis=-1, keepdims=True) + jnp.exp(sink - m)
    o = jnp.dot(p.astype(BF16), v, preferred_element_type=F32) / den
    for g in range(KV_GROUP):
        o_ref[:, g * HEAD_DIM:(g + 1) * HEAD_DIM] = o[g * BLOCK:(g + 1) * BLOCK].astype(o_ref.dtype)


def _attention(qkv, bias, sink):
    t = qkv.shape[0]
    n_heads = D_MODEL // HEAD_DIM
    n_kv = n_heads // KV_GROUP
    nblk = t // BLOCK
    flags = np.ones((nblk, 2), np.int32)
    start = 0
    for ln in _seq_lens():
        flags[start // BLOCK, 0] = 0
        flags[(start + ln) // BLOCK - 1, 1] = 0
        start += ln
    flags = jnp.asarray(flags.reshape(-1))
    kcol = n_heads
    vcol = n_heads + n_kv
    gw = KV_GROUP * HEAD_DIM

    def prev(i):
        return jnp.maximum(i - 1, 0)

    def nxt(i):
        return jnp.minimum(i + 1, nblk - 1)

    blk = (BLOCK, HEAD_DIM)
    return pl.pallas_call(
        _attention_kernel,
        out_shape=jax.ShapeDtypeStruct((t, n_heads * HEAD_DIM), BF16),
        grid_spec=pltpu.PrefetchScalarGridSpec(
            num_scalar_prefetch=1,
            grid=(n_kv, nblk),
            in_specs=[
                pl.BlockSpec(memory_space=pltpu.SMEM),
                pl.BlockSpec((BLOCK, gw), lambda kv, i, f: (i, kv)),
                pl.BlockSpec(blk, lambda kv, i, f: (prev(i), kcol + kv)),
                pl.BlockSpec(blk, lambda kv, i, f: (i, kcol + kv)),
                pl.BlockSpec(blk, lambda kv, i, f: (nxt(i), kcol + kv)),
                pl.BlockSpec(blk, lambda kv, i, f: (prev(i), vcol + kv)),
                pl.BlockSpec(blk, lambda kv, i, f: (i, vcol + kv)),
                pl.BlockSpec(blk, lambda kv, i, f: (nxt(i), vcol + kv)),
                pl.BlockSpec((None, KV_GROUP, BLOCK, 3 * BLOCK), lambda kv, i, f: (kv, 0, 0, 0)),
            ],
            out_specs=pl.BlockSpec((BLOCK, gw), lambda kv, i, f: (i, kv)),
        ),
        compiler_params=_params(("arbitrary", "arbitrary")),
        name="window_attention",
    )(flags, sink.astype(F32), qkv, qkv, qkv, qkv, qkv, qkv, qkv, bias)


BF16_ROWS = 16


def _conv_kernel(cur_ref, prev_ref, next_ref, w_ref, b_ref, o_ref, ext_ref, *, tm, starts, ends):
    row0 = pl.program_id(0) * tm
    is_first = functools.reduce(jnp.logical_or, [row0 == s for s in starts])
    is_last = functools.reduce(jnp.logical_or, [row0 + tm == e for e in ends])
    halo = SUBLANES
    prev = prev_ref[...].astype(F32)[BF16_ROWS - halo:, :]
    nxt = next_ref[...].astype(F32)[:halo, :]
    ext_ref[0:halo, :] = jnp.where(is_first, 0.0, prev)
    ext_ref[halo:halo + tm, :] = cur_ref[...].astype(F32)
    ext_ref[halo + tm:2 * halo + tm, :] = jnp.where(is_last, 0.0, nxt)
    half = CONV_WIDTH // 2
    w = w_ref[...]
    acc = jnp.broadcast_to(b_ref[...], o_ref.shape).astype(F32)
    ext = ext_ref[...]
    n = ext.shape[0]
    for k in range(CONV_WIDTH):
        off = halo - half + k
        shift = off % SUBLANES
        src = ext if shift == 0 else pltpu.roll(ext, n - shift, axis=0)
        acc = acc + w[k:k + 1, :] * src[off - shift:off - shift + tm, :]
    o_ref[...] = jax.nn.silu(acc).astype(o_ref.dtype)


def _conv_silu(zxbc, conv_w, conv_b, layer, col0, ncols, tm, tc):
    t = zxbc.shape[0]
    starts, ends = [], []
    s = 0
    for ln in _seq_lens():
        starts.append(s)
        ends.append(s + ln)
        s += ln
    cb0 = col0 // tc
    hb = tm // BF16_ROWS
    last_hb = t // BF16_ROWS - 1
    kern = functools.partial(_conv_kernel, tm=tm, starts=tuple(starts), ends=tuple(ends))
    return pl.pallas_call(
        kern,
        out_shape=jax.ShapeDtypeStruct((t, ncols), BF16),
        grid=(t // tm, ncols // tc),
        in_specs=[
            pl.BlockSpec((tm, tc), lambda i, j: (i, cb0 + j)),
            pl.BlockSpec((BF16_ROWS, tc), lambda i, j: (jnp.maximum(i * hb - 1, 0), cb0 + j)),
            pl.BlockSpec((BF16_ROWS, tc), lambda i, j: (jnp.minimum((i + 1) * hb, last_hb), cb0 + j)),
            pl.BlockSpec((None, CONV_WIDTH, tc), lambda i, j: (layer, 0, j)),
            pl.BlockSpec((None, 1, tc), lambda i, j: (layer, 0, j)),
        ],
        out_specs=pl.BlockSpec((tm, tc), lambda i, j: (i, j)),
        scratch_shapes=[pltpu.VMEM((tm + 2 * SUBLANES, tc), F32)],
        compiler_params=_params(("arbitrary", "arbitrary")),
        name="conv_silu",
    )(zxbc, zxbc, zxbc, conv_w, conv_b.reshape(conv_b.shape[0], 1, conv_b.shape[1]))


def _split3(a):
    a1 = a.astype(BF16)
    r = a - a1.astype(F32)
    a2 = r.astype(BF16)
    a3 = (r - a2.astype(F32)).astype(BF16)
    return a1, a2, a3


def _nt(a, b):
    return lax.dot_general(a, b, (((1,), (1,)), ((), ())), preferred_element_type=F32)


def _ssd_chunk_setup(dt_ref, dtb_ref, a_ref, grow_ref, dtrow_ref, wst_ref, eoff_ref, cd_ref):
    c = SSD_CHUNK
    nh = dtb_ref.shape[1] // 2
    x = dt_ref[...] + dtb_ref[...]
    dt = jnp.maximum(x, 0.0) + jnp.log1p(jnp.exp(-jnp.abs(x)))
    dt_row = dt.T
    a_row = dt_row * a_ref[...]
    ii = lax.broadcasted_iota(jnp.int32, (c, c), 0)
    jj = lax.broadcasted_iota(jnp.int32, (c, c), 1)
    tl = (jj <= ii).astype(BF16)
    tu = (jj >= ii).astype(BF16)
    parts = _split3(a_row)
    p_row = sum(_nt(p, tl) for p in parts)
    r_row = sum(_nt(p, tu) for p in parts)
    hrow = lax.broadcasted_iota(jnp.int32, a_row.shape, 0)
    fwd = hrow < nh
    g_row = jnp.where(fwd, p_row, r_row)
    g_end = jnp.where(fwd, g_row[:, c - 1:c], g_row[:, 0:1])
    grow_ref[...] = g_row
    dtrow_ref[...] = dt_row
    wst_ref[...] = jnp.exp(g_end - g_row) * dt_row
    eoff_ref[...] = jnp.exp(g_row)
    cd_ref[...] = jnp.broadcast_to(jnp.exp(g_end), cd_ref.shape)


def _ssd_kernel(order_ref, reset_ref, dt_ref, dtb_ref, a_ref, x_ref, b_ref, c_ref, *rest, backward,
                hpg, nh):
    if backward:
        (o_ref, grow_ref, dtrow_ref, wst_ref, eoff_ref, cd_ref, s_ref) = rest
    else:
        (z_ref, yb_ref, dskip_ref, nw_ref, o_ref,
         grow_ref, dtrow_ref, wst_ref, eoff_ref, cd_ref, s_ref) = rest
    del order_ref
    c = SSD_CHUNK
    p = SSD_HEAD_DIM
    gw = hpg * p
    d = 1 if backward else 0

    _ssd_chunk_setup(dt_ref, dtb_ref, a_ref, grow_ref, dtrow_ref, wst_ref, eoff_ref, cd_ref)

    @pl.when(reset_ref[pl.program_id(0)] > 0)
    def _():
        s_ref[...] = jnp.zeros(s_ref.shape, F32)

    si = lax.broadcasted_iota(jnp.int32, (c, c), 0)
    li = lax.broadcasted_iota(jnp.int32, (c, c), 1)

    for g in range(SSD_GROUPS):
        cols = slice(g * gw, (g + 1) * gw)
        ncols = slice(g * D_STATE, (g + 1) * D_STATE)

        def rows(ref, dd, g=g):
            r0 = dd * nh + g * hpg
            return ref[r0:r0 + hpg, :]

        x_t = x_ref[:, cols].astype(F32).T
        bmat = b_ref[:, ncols]
        cmat = c_ref[:, ncols]

        state = s_ref[g]
        eoff = rows(eoff_ref, d)
        wst = rows(wst_ref, d)
        cdr = rows(cd_ref, d)
        y_off = _nt(state.astype(BF16), cmat)
        xw, scaled_state, y_parts = [], [], []
        for r in range(hpg):
            sl = slice(r * p, (r + 1) * p)
            y_parts.append(y_off[sl] * eoff[r:r + 1, :])
            xw.append((x_t[sl] * wst[r:r + 1, :]).astype(BF16))
            scaled_state.append(state[sl] * cdr[r:r + 1, :])
        s_ref[g] = jnp.concatenate(scaled_state, axis=0) + jnp.dot(
            jnp.concatenate(xw, axis=0), bmat, preferred_element_type=F32)
        y_t = jnp.concatenate(y_parts, axis=0)

        if backward:
            o_ref[:, cols] = y_t.T.astype(o_ref.dtype)
            continue

        cbt = _nt(bmat, cmat)
        grow = (rows(grow_ref, 0), rows(grow_ref, 1))
        dtrow = (rows(dtrow_ref, 0), rows(dtrow_ref, 1))
        yd = []
        for r in range(hpg):
            sl = slice(r * p, (r + 1) * p)
            acc = None
            for dd in (0, 1):
                g_l = grow[dd][r:r + 1, :]
                g_s = jnp.broadcast_to(g_l, (c, c)).T
                causal = (si <= li) if dd == 0 else (si >= li)
                m_t = cbt * jnp.exp(jnp.where(causal, g_l - g_s, -jnp.inf))
                xd = (x_t[sl] * dtrow[dd][r:r + 1, :]).astype(BF16)
                term = jnp.dot(xd, m_t.astype(BF16), preferred_element_type=F32)
                acc = term if acc is None else acc + term
            yd.append(acc)
        y_t = y_t + jnp.concatenate(yd, axis=0) + dskip_ref[cols, :] * x_t
        y = y_t.T + yb_ref[:, cols].astype(F32)
        gated = y * jax.nn.silu(z_ref[:, cols].astype(F32))
        gated = gated * lax.rsqrt(jnp.mean(gated * gated, axis=-1, keepdims=True) + GATED_NORM_EPS)
        o_ref[:, cols] = (gated * nw_ref[:, cols]).astype(o_ref.dtype)


def _ssd_scan(zxbc, xbc, dt_raw, dt_bias, a_log, d_skip, norm_w, layer):
    t = xbc.shape[0]
    d_inner = SSD_EXPAND * D_MODEL
    nh = d_inner // SSD_HEAD_DIM
    hpg = nh // SSD_GROUPS
    gw = hpg * SSD_HEAD_DIM
    c = SSD_CHUNK
    nchunks = t // c
    gn = SSD_GROUPS * D_STATE
    assert D_STATE == c and d_inner % gn == 0

    fwd_order, bwd_order, reset = [], [], []
    s = 0
    for ln in _seq_lens():
        ids = list(range(s // c, (s + ln) // c))
        fwd_order += ids
        bwd_order += ids[::-1]
        reset += [1] + [0] * (len(ids) - 1)
        s += ln
    reset = jnp.asarray(np.asarray(reset, np.int32))

    dtb = dt_bias[layer].astype(F32).reshape(1, 2 * nh)
    a_neg = -jnp.exp(a_log[layer].astype(F32)).reshape(2 * nh, 1)
    a_rep = jnp.broadcast_to(a_neg, (2 * nh, c))
    dskip = jnp.broadcast_to(jnp.repeat(d_skip[layer].astype(F32), SSD_HEAD_DIM)[:, None], (d_inner, c))
    nw = norm_w[layer].reshape(1, d_inner)

    bblk = d_inner // gn
    tab = pltpu.VMEM((2 * nh, c), F32)
    state = pltpu.VMEM((SSD_GROUPS, gw, D_STATE), F32)

    def common_specs():
        return [
            pl.BlockSpec((c, 2 * nh), lambda i, o, r: (o[i], 0)),
            pl.BlockSpec((1, 2 * nh), lambda i, o, r: (0, 0)),
            pl.BlockSpec((2 * nh, c), lambda i, o, r: (0, 0)),
            pl.BlockSpec((c, d_inner), lambda i, o, r: (o[i], 0)),
            pl.BlockSpec((c, gn), lambda i, o, r: (o[i], bblk)),
            pl.BlockSpec((c, gn), lambda i, o, r: (o[i], bblk + 1)),
        ]

    y_bwd = pl.pallas_call(
        functools.partial(_ssd_kernel, backward=True, hpg=hpg, nh=nh),
        out_shape=jax.ShapeDtypeStruct((t, d_inner), BF16),
        grid_spec=pltpu.PrefetchScalarGridSpec(
            num_scalar_prefetch=2,
            grid=(nchunks,),
            in_specs=common_specs(),
            out_specs=pl.BlockSpec((c, d_inner), lambda i, o, r: (o[i], 0)),
            scratch_shapes=[tab, tab, tab, tab, tab, state],
        ),
        compiler_params=_params(("arbitrary",)),
        name="ssd_backward",
    )(jnp.asarray(np.asarray(bwd_order, np.int32)), reset, dt_raw, dtb, a_rep, xbc, xbc, xbc)

    return pl.pallas_call(
        functools.partial(_ssd_kernel, backward=False, hpg=hpg, nh=nh),
        out_shape=jax.ShapeDtypeStruct((t, d_inner), BF16),
        grid_spec=pltpu.PrefetchScalarGridSpec(
            num_scalar_prefetch=2,
            grid=(nchunks,),
            in_specs=common_specs() + [
                pl.BlockSpec((c, d_inner), lambda i, o, r: (o[i], 0)),
                pl.BlockSpec((c, d_inner), lambda i, o, r: (o[i], 0)),
                pl.BlockSpec((d_inner, c), lambda i, o, r: (0, 0)),
                pl.BlockSpec((1, d_inner), lambda i, o, r: (0, 0)),
            ],
            out_specs=pl.BlockSpec((c, d_inner), lambda i, o, r: (o[i], 0)),
            scratch_shapes=[tab, tab, tab, tab, tab, state],
        ),
        compiler_params=_params(("arbitrary",)),
        name="ssd_forward",
    )(jnp.asarray(np.asarray(fwd_order, np.int32)), reset, dt_raw, dtb, a_rep, xbc, xbc, xbc,
      zxbc, y_bwd, dskip, nw)


HI16 = 0xFFFF0000


def _pack_bf16_pair(a, b):
    ua = pltpu.bitcast(a.astype(BF16).astype(F32), jnp.uint32)
    ub = pltpu.bitcast(b.astype(BF16).astype(F32), jnp.uint32)
    return ua | lax.shift_right_logical(ub, jnp.uint32(16))


def _unpack_bf16_pair(w):
    a = pltpu.bitcast(w & jnp.uint32(HI16), F32).astype(BF16)
    b = pltpu.bitcast(lax.shift_left(w, jnp.uint32(16)), F32).astype(BF16)
    return a, b


def _router_kernel(x_ref, nw_ref, wr_ref, xa_ref, *, n_exp):
    x = x_ref[...]
    half = x.shape[1] // 2
    ms = jnp.mean(x * x, axis=-1, keepdims=True)
    hn = (x * lax.rsqrt(ms + NORM_EPS)) * nw_ref[...]
    h1 = hn.astype(BF16)
    h2 = (hn - h1.astype(F32)).astype(BF16)
    w = wr_ref[...]
    w1 = w.astype(BF16)
    w2 = (w - w1.astype(F32)).astype(BF16)
    logits = (jnp.dot(h1, w1, preferred_element_type=F32) + jnp.dot(h1, w2, preferred_element_type=F32)
              + jnp.dot(h2, w1, preferred_element_type=F32))
    lane = lax.broadcasted_iota(jnp.int32, logits.shape, 1)
    logits = jnp.where(lane < n_exp, logits, -jnp.inf)
    m = jnp.max(logits, axis=-1, keepdims=True)
    e = jnp.exp(logits - m)
    xa_ref[:, :half] = _pack_bf16_pair(hn[:, :half], hn[:, half:])
    xa_ref[:, half:] = pltpu.bitcast(e / jnp.sum(e, axis=-1, keepdims=True), jnp.uint32)


def _router(x, nw, wr, layer, tm):
    t, d = x.shape
    n_exp = wr.shape[-1]
    wr = jnp.pad(wr, ((0, 0), (0, 0), (0, LANES - n_exp)))
    return pl.pallas_call(
        functools.partial(_router_kernel, n_exp=n_exp),
        out_shape=jax.ShapeDtypeStruct((t, d // 2 + LANES), jnp.uint32),
        grid=(t // tm,),
        in_specs=[
            pl.BlockSpec((tm, d), lambda i: (i, 0)),
            pl.BlockSpec((None, 1, d), lambda i: (layer, 0, 0)),
            pl.BlockSpec((None, d, LANES), lambda i: (layer, 0, 0)),
        ],
        out_specs=pl.BlockSpec((tm, d // 2 + LANES), lambda i: (i, 0)),
        compiler_params=_params(("arbitrary",)),
        name="router",
    )(x, nw.reshape(nw.shape[0], 1, d), wr)


def _route_kernel(aff_ref, idx_ref, bits_ref, sel_ref, cnt_ref, *, cap, n_tok, tok0, jb):
    nt, ne, _ = aff_ref.shape
    bits_ref[...] = pltpu.bitcast(aff_ref[...], jnp.int32)

    def count_ge(cand):
        def body(ti, acc):
            return acc + (bits_ref[ti] >= cand).astype(jnp.int32)
        acc = lax.fori_loop(0, nt, body, jnp.zeros((ne, LANES), jnp.int32))
        return jnp.sum(acc, axis=1, keepdims=True)

    def bisect(b, prefix):
        cand = prefix | lax.shift_left(jnp.int32(1), 30 - b)
        return jnp.where(count_ge(cand) >= cap, cand, prefix)

    thr = lax.fori_loop(0, 31, bisect, jnp.zeros((ne, 1), jnp.int32))

    def count_gt(ti, acc):
        return acc + (bits_ref[ti] > thr).astype(jnp.int32)
    n_gt = jnp.sum(lax.fori_loop(0, nt, count_gt, jnp.zeros((ne, LANES), jnp.int32)), axis=1, keepdims=True)
    need = (cap - n_gt).astype(F32)

    ii = lax.broadcasted_iota(jnp.int32, (LANES, LANES), 0)
    jj = lax.broadcasted_iota(jnp.int32, (LANES, LANES), 1)
    excl = (ii < jj).astype(BF16)

    def tie_body(ti, run):
        b = bits_ref[ti]
        eq = (b == thr)
        rank = run + jnp.dot(eq.astype(BF16), excl, preferred_element_type=F32)
        sel = jnp.logical_or(b > thr, jnp.logical_and(eq, rank < need))
        sel_ref[ti] = sel.astype(F32)
        return run + jnp.sum(eq.astype(F32), axis=1, keepdims=True)
    lax.fori_loop(0, nt, tie_body, jnp.zeros((ne, 1), F32))

    def cnt_body(ti, run):
        s = sel_ref[ti]
        inc = run + jnp.dot(s.astype(BF16), excl, preferred_element_type=F32) + s
        cnt_ref[ti] = inc
        return run + jnp.sum(s, axis=1, keepdims=True)
    lax.fori_loop(0, nt, cnt_body, jnp.zeros((ne, 1), F32))

    ones = jnp.ones((SUBLANES, LANES), BF16)
    for e in range(ne):
        for j0 in range(0, cap, jb):
            jcol = (lax.broadcasted_iota(jnp.int32, (jb, LANES), 0) + j0).astype(F32)

            def body(ti, acc, e=e, jcol=jcol):
                row = cnt_ref[ti][e:e + 1, :]
                return acc + (row <= jcol).astype(F32)
            acc = lax.fori_loop(0, nt, body, jnp.zeros((jb, LANES), F32))
            tok = _nt(ones, acc.astype(BF16))
            idx_ref[e:e + 1, j0:j0 + jb] = tok[0:1, :].astype(jnp.int32) + tok0


def _route(aff3, cap, tok0):
    nt, ne, _ = aff3.shape
    jb = min(cap, 256)
    return pl.pallas_call(
        functools.partial(_route_kernel, cap=cap, n_tok=nt * LANES, tok0=tok0, jb=jb),
        out_shape=jax.ShapeDtypeStruct((ne, cap), jnp.int32),
        scratch_shapes=[pltpu.VMEM((nt, ne, LANES), jnp.int32),
                        pltpu.VMEM((nt, ne, LANES), F32),
                        pltpu.VMEM((nt, ne, LANES), F32)],
        compiler_params=pltpu.CompilerParams(vmem_limit_bytes=VMEM_LIMIT),
        name="route_topk",
    )(aff3)


def _gather_rows(idx_ref, base, n, src_hbm, dst_ref, sem):
    def issue(j, carry):
        tok = idx_ref[base + j]
        pltpu.make_async_copy(src_hbm.at[pl.ds(tok, 1)], dst_ref.at[pl.ds(j, 1)], sem).start()
        return carry
    lax.fori_loop(0, n, issue, 0, unroll=DMA_ISSUE_UNROLL)
    pltpu.make_async_copy(src_hbm.at[pl.ds(0, n)], dst_ref, sem).wait()


def _ffn_kernel(idx_ref, xa_hbm, wg_ref, wu_ref, wd_ref, o_ref, xg_ref, xb_ref, gate_ref, h_ref,
                wgu_ref, wdb_ref, sem, *, cap, d, n_exp, nf):
    e = pl.program_id(0)
    st = pl.program_id(1)
    s = pl.program_id(2)
    half = d // 2

    @pl.when(st == 0)
    def _():
        _gather_rows(idx_ref, (s * n_exp + e) * cap, cap, xa_hbm, xg_ref, sem)
        a, b = _unpack_bf16_pair(xg_ref[:, :half])
        xb_ref[s] = jnp.concatenate([a, b], axis=1)
        aff = pltpu.bitcast(xg_ref[:, half:], F32)
        lane = lax.broadcasted_iota(jnp.int32, aff.shape, 1)
        gate_ref[s] = jnp.sum(jnp.where(lane == e, aff, 0.0), axis=1, keepdims=True)

    @pl.when(jnp.logical_and(st < nf, s == 0))
    def _():
        wgu_ref[...] = jnp.concatenate([wg_ref[...].astype(BF16), wu_ref[...].astype(BF16)], axis=1)

    @pl.when(st < nf)
    def _():
        tf = wg_ref.shape[1]
        gu = jnp.dot(xb_ref[s], wgu_ref[...], preferred_element_type=F32)
        h_ref[s, st] = (jax.nn.silu(gu[:, :tf]) * gu[:, tf:]).astype(BF16)

    @pl.when(jnp.logical_and(st >= nf, s == 0))
    def _():
        wdb_ref[...] = wd_ref[...].astype(BF16)

    @pl.when(st >= nf)
    def _():
        h = jnp.concatenate([h_ref[s, f] for f in range(nf)], axis=1)
        y = jnp.dot(h, wdb_ref[...], preferred_element_type=F32)
        o_ref[...] = y * gate_ref[s]


def _expert_ffn(idx, xa, wg, wu, wd, layer, cap, tf, tn):
    n_sets = len(_set_lens())
    d = D_MODEL
    n_exp = N_EXPERTS
    ff = wg.shape[-1]
    nf = ff // tf

    def f_of(st):
        return jnp.minimum(st, nf - 1)

    def n_of(st):
        return jnp.maximum(st - nf, 0)

    return pl.pallas_call(
        functools.partial(_ffn_kernel, cap=cap, d=d, n_exp=n_exp, nf=nf),
        out_shape=jax.ShapeDtypeStruct((n_sets, n_exp, cap, d), F32),
        grid_spec=pltpu.PrefetchScalarGridSpec(
            num_scalar_prefetch=1,
            grid=(n_exp, nf + d // tn, n_sets),
            in_specs=[
                pl.BlockSpec(memory_space=pl.ANY),
                pl.BlockSpec((None, None, d, tf), lambda e, st, s, idx: (layer, e, 0, f_of(st))),
                pl.BlockSpec((None, None, d, tf), lambda e, st, s, idx: (layer, e, 0, f_of(st))),
                pl.BlockSpec((None, None, ff, tn), lambda e, st, s, idx: (layer, e, 0, n_of(st))),
            ],
            out_specs=pl.BlockSpec((None, None, cap, tn),
                                   lambda e, st, s, idx: (jnp.where(st < nf, 0, s), e, 0, n_of(st))),
            scratch_shapes=[pltpu.VMEM((cap, d // 2 + LANES), jnp.uint32),
                            pltpu.VMEM((n_sets, cap, d), BF16),
                            pltpu.VMEM((n_sets, cap, 1), F32),
                            pltpu.VMEM((n_sets, nf, cap, tf), BF16),
                            pltpu.VMEM((d, 2 * tf), BF16),
                            pltpu.VMEM((ff, tn), BF16),
                            pltpu.SemaphoreType.DMA],
        ),
        compiler_params=_params(("arbitrary", "arbitrary", "arbitrary")),
        name="expert_ffn",
    )(idx, xa, wg, wu, wd)


def _combine_kernel(idx_ref, ye_ref, x_in, x_hbm, buf_ref, sem_in, sem_out, *, cap, n_exp):
    del x_in
    s = pl.program_id(0)
    e = pl.program_id(1)
    base = (s * n_exp + e) * cap

    hc = cap // 2
    halves = ((0, sem_in.at[0], sem_out.at[0]), (hc, sem_in.at[1], sem_out.at[1]))

    def gather(j0, sem):
        def body(j, carry):
            tok = idx_ref[base + j]
            pltpu.make_async_copy(x_hbm.at[pl.ds(tok, 1)], buf_ref.at[pl.ds(j, 1)], sem).start()
            return carry
        lax.fori_loop(j0, j0 + hc, body, 0, unroll=DMA_ISSUE_UNROLL)

    def scatter(j0, sem):
        def body(j, carry):
            tok = idx_ref[base + j]
            pltpu.make_async_copy(buf_ref.at[pl.ds(j, 1)], x_hbm.at[pl.ds(tok, 1)], sem).start()
            return carry
        lax.fori_loop(j0, j0 + hc, body, 0, unroll=DMA_ISSUE_UNROLL)

    for j0, si, _ in halves:
        gather(j0, si)
    for j0, si, so in halves:
        rows = pl.ds(j0, hc)
        pltpu.make_async_copy(x_hbm.at[pl.ds(0, hc)], buf_ref.at[rows], si).wait()
        buf_ref[rows, :] = buf_ref[rows, :] + ye_ref[rows, :]
        scatter(j0, so)
    for j0, _, so in halves:
        pltpu.make_async_copy(buf_ref.at[pl.ds(j0, hc)], x_hbm.at[pl.ds(0, hc)], so).wait()


def _combine(idx, ye, x, cap):
    n_sets, n_exp = ye.shape[:2]
    d = x.shape[1]
    return pl.pallas_call(
        functools.partial(_combine_kernel, cap=cap, n_exp=n_exp),
        out_shape=jax.ShapeDtypeStruct(x.shape, x.dtype),
        grid_spec=pltpu.PrefetchScalarGridSpec(
            num_scalar_prefetch=1,
            grid=(n_sets, n_exp),
            in_specs=[
                pl.BlockSpec((None, None, cap, d), lambda s, e, idx: (s, e, 0, 0)),
                pl.BlockSpec(memory_space=pl.ANY),
            ],
            out_specs=pl.BlockSpec(memory_space=pl.ANY),
            scratch_shapes=[pltpu.VMEM((cap, d), F32), pltpu.SemaphoreType.DMA((2,)),
                            pltpu.SemaphoreType.DMA((2,))],
        ),
        input_output_aliases={2: 0},
        compiler_params=_params(("arbitrary", "arbitrary")),
        name="expert_combine",
    )(idx, ye, x)


def _expert_choice_ffn(x, ffn_norm_w, router_w, wg, wu, wd, layer):
    t = x.shape[0]
    xa = _router(x, ffn_norm_w, router_w, layer, tm=min(512, t))
    half = x.shape[1] // 2
    aff = lax.bitcast_convert_type(xa[:, half:half + N_EXPERTS], F32)
    idx = []
    tok0 = 0
    cap = None
    for n in _set_lens():
        assert cap is None or cap == CAPACITY_FACTOR * n // N_EXPERTS
        cap = CAPACITY_FACTOR * n // N_EXPERTS
        a3 = aff[tok0:tok0 + n].reshape(n // LANES, LANES, N_EXPERTS).transpose(0, 2, 1)
        idx.append(_route(a3, cap, tok0))
        tok0 += n
    idx = jnp.stack(idx).reshape(-1)
    ye = _expert_ffn(idx, xa, wg, wu, wd, layer, cap, tf=_tile(wg.shape[-1], (256, 128)),
                     tn=_tile(x.shape[1], (256, 128)))
    return _combine(idx, ye, x, cap)


def _final_norm_kernel(x_ref, nw_ref, o_ref):
    x = x_ref[...]
    ms = jnp.mean(x * x, axis=-1, keepdims=True)
    o_ref[...] = (x * lax.rsqrt(ms + NORM_EPS)) * nw_ref[...]


def _final_norm(x, nw, row0, nrows, tm):
    d = x.shape[1]
    rb0 = row0 // tm
    return pl.pallas_call(
        _final_norm_kernel,
        out_shape=jax.ShapeDtypeStruct((nrows, d), F32),
        grid=(nrows // tm,),
        in_specs=[pl.BlockSpec((tm, d), lambda i: (rb0 + i, 0)),
                  pl.BlockSpec((1, d), lambda i: (0, 0))],
        out_specs=pl.BlockSpec((tm, d), lambda i: (i, 0)),
        compiler_params=_params(("arbitrary",)),
        name="final_norm",
    )(x, nw.reshape(1, d))


def kernel(x_prompt, x_sample, rel_bias, mixer_norm_w, ffn_norm_w, final_norm_w, attn_w_qkv, attn_w_o, attn_sink, ssd_w_in, ssd_conv_w, ssd_conv_b, ssd_dt_bias, ssd_A_log, ssd_D, ssd_norm_w, ssd_w_out, router_w, expert_w_gate, expert_w_up, expert_w_down):
    d = D_MODEL
    x = jnp.concatenate([x_prompt.reshape(-1, d), x_sample.reshape(-1, d)], axis=0)
    t = x.shape[0]
    n_heads = d // HEAD_DIM
    q_dim = n_heads * HEAD_DIM
    qkv_dim = q_dim + 2 * (n_heads // KV_GROUP) * HEAD_DIM
    d_inner = SSD_EXPAND * d
    nh = d_inner // SSD_HEAD_DIM
    conv_dim = d_inner + 2 * SSD_GROUPS * D_STATE
    tm = min(1024, t)
    bias = _t5_bias_table(rel_bias)

    for i in range(DEPTH):
        j = i // N_MIXERS
        hn = _rmsnorm(x, mixer_norm_w, i, tm)
        if i % N_MIXERS == 0:
            qkv = _matmul(hn, attn_w_qkv, j, 0, qkv_dim, _tile(qkv_dim, WIDE_TILES), BF16, tm)
            a = _attention(qkv, bias, attn_sink[j])
            x = _matmul(a, attn_w_o, j, 0, d, _tile(d), F32, tm, x=x)
        else:
            zx_dim = d_inner + conv_dim
            zxbc = _matmul(hn, ssd_w_in, j, 0, zx_dim, _tile(zx_dim, WIDE_TILES), BF16, tm)
            dt_raw = _matmul(hn, ssd_w_in, j, zx_dim, 2 * nh, 2 * nh, F32, tm)
            xbc = _conv_silu(zxbc, ssd_conv_w, ssd_conv_b, j, d_inner, conv_dim, min(512, SEQ),
                             _tile(math.gcd(d_inner, conv_dim)))
            y = _ssd_scan(zxbc, xbc, dt_raw, ssd_dt_bias, ssd_A_log, ssd_D, ssd_norm_w, j)
            x = _matmul(y, ssd_w_out, j, 0, d, _tile(d), F32, tm, x=x)
        x = _expert_choice_ffn(x, ffn_norm_w, router_w, expert_w_gate, expert_w_up, expert_w_down, i)

    n_prompt = BATCH * SEQ
    n_sample = DEC_BATCH * DEC_SEQ
    y_prompt = _final_norm(x, final_norm_w, 0, n_prompt, min(1024, n_prompt))
    y_sample = _final_norm(x, final_norm_w, n_prompt, n_sample, min(1024, n_prompt))
    return (y_prompt.reshape(BATCH, SEQ, d), y_sample.reshape(DEC_BATCH, DEC_SEQ, d))
```

```python
import functools
import math

import jax
import jax.numpy as jnp
import numpy as np
from jax import lax
from jax.experimental import pallas as pl
from jax.experimental.pallas import tpu as pltpu

D_MODEL = 2048
BATCH = 2
SEQ = 4096
DEPTH = 4
DEC_BATCH = 1
DEC_SEQ = 8192

N_MIXERS = 2

HEAD_DIM = 128
WINDOW = 128
BLOCK = 128
REL_BUCKETS = 32
REL_MAX_DIST = 128
KV_GROUP = 4

SSD_EXPAND = 2
SSD_HEAD_DIM = 64
SSD_GROUPS = 8
D_STATE = 128
CONV_WIDTH = 7
SSD_CHUNK = 128

N_EXPERTS = 16
FF_MULT = 2
CAPACITY_FACTOR = 2

NORM_EPS = 1e-6
GATED_NORM_EPS = 1e-5
NEG_INF = -1e30

LANES = 128
SUBLANES = 8
VMEM_LIMIT = 60 * 1024 * 1024
DMA_ISSUE_UNROLL = 8

F32 = jnp.float32
BF16 = jnp.bfloat16


def _seq_lens():
    return [SEQ] * BATCH + [DEC_SEQ] * DEC_BATCH


def _set_lens():
    return [BATCH * SEQ, DEC_BATCH * DEC_SEQ]


WIDE_TILES = (1024, 768, 512, 384, 256, 128)


def _tile(n, cands=(512, 384, 256, 128)):
    return next(c for c in cands if n % c == 0)


def _params(sem):
    return pltpu.CompilerParams(dimension_semantics=sem, vmem_limit_bytes=VMEM_LIMIT)


def _rmsnorm_kernel(x_ref, nw_ref, o_ref):
    x = x_ref[...]
    ms = jnp.mean(x * x, axis=-1, keepdims=True)
    o_ref[...] = ((x * lax.rsqrt(ms + NORM_EPS)) * nw_ref[...]).astype(o_ref.dtype)


def _rmsnorm(x, nw, layer, tm):
    t, d = x.shape
    return pl.pallas_call(
        _rmsnorm_kernel,
        out_shape=jax.ShapeDtypeStruct((t, d), BF16),
        grid=(t // tm,),
        in_specs=[pl.BlockSpec((tm, d), lambda i: (i, 0)),
                  pl.BlockSpec((None, 1, d), lambda i: (layer, 0, 0))],
        out_specs=pl.BlockSpec((tm, d), lambda i: (i, 0)),
        compiler_params=_params(("arbitrary",)),
        name="rmsnorm",
    )(x, nw.reshape(nw.shape[0], 1, d))


def _matmul_kernel(a_ref, w_ref, *rest, residual):
    if residual:
        x_ref, o_ref, wb_ref = rest
    else:
        o_ref, wb_ref = rest

    @pl.when(pl.program_id(1) == 0)
    def _():
        wb_ref[...] = w_ref[...].astype(BF16)

    y = jnp.dot(a_ref[...], wb_ref[...], preferred_element_type=F32)
    if residual:
        y = x_ref[...] + y
    o_ref[...] = y.astype(o_ref.dtype)


def _matmul(a, w, wl, col0, ncols, tn, out_dtype, tm, x=None):
    t, k = a.shape
    cb0 = col0 // tn
    in_specs = [
        pl.BlockSpec((tm, k), lambda j, i: (i, 0)),
        pl.BlockSpec((None, k, tn), lambda j, i: (wl, 0, cb0 + j)),
    ]
    args = [a, w]
    if x is not None:
        in_specs.append(pl.BlockSpec((tm, tn), lambda j, i: (i, j)))
        args.append(x)
    return pl.pallas_call(
        functools.partial(_matmul_kernel, residual=x is not None),
        out_shape=jax.ShapeDtypeStruct((t, ncols), out_dtype),
        grid=(ncols // tn, t // tm),
        in_specs=in_specs,
        out_specs=pl.BlockSpec((tm, tn), lambda j, i: (i, j)),
        scratch_shapes=[pltpu.VMEM((k, tn), BF16)],
        compiler_params=_params(("arbitrary", "arbitrary")),
        name="matmul_residual" if x is not None else "matmul",
    )(*args)


def _t5_bias_table(rel_bias):
    n_heads = D_MODEL // HEAD_DIM
    qi = np.arange(BLOCK)[:, None]
    s = np.arange(3 * BLOCK)[None, :]
    rel = s - BLOCK - qi
    nb = REL_BUCKETS // 2
    ret = (rel > 0).astype(np.int32) * nb
    n = np.abs(rel)
    max_exact = nb // 2
    large = max_exact + (np.log(np.maximum(n, 1) / max_exact)
                         / np.log(REL_MAX_DIST / max_exact) * (nb - max_exact)).astype(np.int32)
    large = np.minimum(large, nb - 1)
    buckets = ret + np.where(n < max_exact, n, large)
    onehot = jnp.asarray(np.eye(REL_BUCKETS, dtype=np.float32)[buckets.reshape(-1)])
    bias = jnp.einsum("pb,bh->hp", onehot, rel_bias.astype(F32), precision=lax.Precision.HIGHEST)
    bias = bias.reshape(n_heads, BLOCK, 3 * BLOCK)
    bias = jnp.where(jnp.asarray(np.abs(rel) <= WINDOW)[None], bias, NEG_INF)
    return bias.reshape(n_heads // KV_GROUP, KV_GROUP, BLOCK, 3 * BLOCK)


def _attention_kernel(flags_ref, sink_ref, q_ref, kp_ref, kc_ref, kn_ref, vp_ref, vc_ref, vn_ref,
                      bias_ref, o_ref):
    n = pl.program_id(0)
    has_prev = flags_ref[2 * n] > 0
    has_next = flags_ref[2 * n + 1] > 0
    n_kv = bias_ref.shape[0]
    rows = KV_GROUP * BLOCK
    col = lax.broadcasted_iota(jnp.int32, (rows, 3 * BLOCK), 1)
    in_seq = jnp.logical_and(jnp.logical_or(col >= BLOCK, has_prev),
                             jnp.logical_or(col < 2 * BLOCK, has_next))
    head = lax.broadcasted_iota(jnp.int32, (rows, 1), 0) // BLOCK
    for kv in range(n_kv):
        hs = slice(kv * HEAD_DIM, (kv + 1) * HEAD_DIM)
        k = jnp.concatenate([kp_ref[:, hs], kc_ref[:, hs], kn_ref[:, hs]], axis=0)
        v = jnp.concatenate([vp_ref[:, hs], vc_ref[:, hs], vn_ref[:, hs]], axis=0)
        q0 = kv * KV_GROUP * HEAD_DIM
        qs = jnp.concatenate([q_ref[:, q0 + g * HEAD_DIM:q0 + (g + 1) * HEAD_DIM] for g in range(KV_GROUP)],
                             axis=0)
        s = lax.dot_general(qs, k, (((1,), (1,)), ((), ())), preferred_element_type=F32)
        s = s * (HEAD_DIM ** -0.5) + bias_ref[kv].reshape(rows, 3 * BLOCK)
        s = jnp.where(in_seq, s, NEG_INF)
        sink = jnp.zeros((rows, 1), F32)
        for g in range(KV_GROUP):
            sink = jnp.where(head == g, sink_ref[kv * KV_GROUP + g], sink)
        m = jnp.maximum(jnp.max(s, axis=-1, keepdims=True), sink)
        p = jnp.exp(s - m)
        den = jnp.sum(p, axis=-1, keepdims=True) + jnp.exp(sink - m)
        o = jnp.dot(p.astype(BF16), v, preferred_element_type=F32) / den
        for g in range(KV_GROUP):
            o_ref[:, q0 + g * HEAD_DIM:q0 + (g + 1) * HEAD_DIM] = o[g * BLOCK:(g + 1) * BLOCK].astype(o_ref.dtype)


def _attention(qkv, bias, sink):
    t = qkv.shape[0]
    n_heads = D_MODEL // HEAD_DIM
    n_kv = n_heads // KV_GROUP
    nblk = t // BLOCK
    flags = np.ones((nblk, 2), np.int32)
    start = 0
    for ln in _seq_lens():
        flags[start // BLOCK, 0] = 0
        flags[(start + ln) // BLOCK - 1, 1] = 0
        start += ln
    flags = jnp.asarray(flags.reshape(-1))
    qw = n_heads * HEAD_DIM
    kw = n_kv * HEAD_DIM
    assert qw % kw == 0
    kcol = qw // kw

    def prev(i):
        return jnp.maximum(i - 1, 0)

    def nxt(i):
        return jnp.minimum(i + 1, nblk - 1)

    blk = (BLOCK, kw)
    return pl.pallas_call(
        _attention_kernel,
        out_shape=jax.ShapeDtypeStruct((t, qw), BF16),
        grid_spec=pltpu.PrefetchScalarGridSpec(
            num_scalar_prefetch=1,
            grid=(nblk,),
            in_specs=[
                pl.BlockSpec(memory_space=pltpu.SMEM),
                pl.BlockSpec((BLOCK, qw), lambda i, f: (i, 0)),
                pl.BlockSpec(blk, lambda i, f: (prev(i), kcol)),
                pl.BlockSpec(blk, lambda i, f: (i, kcol)),
                pl.BlockSpec(blk, lambda i, f: (nxt(i), kcol)),
                pl.BlockSpec(blk, lambda i, f: (prev(i), kcol + 1)),
                pl.BlockSpec(blk, lambda i, f: (i, kcol + 1)),
                pl.BlockSpec(blk, lambda i, f: (nxt(i), kcol + 1)),
                pl.BlockSpec((n_kv, KV_GROUP, BLOCK, 3 * BLOCK), lambda i, f: (0, 0, 0, 0)),
            ],
            out_specs=pl.BlockSpec((BLOCK, qw), lambda i, f: (i, 0)),
        ),
        compiler_params=_params(("arbitrary",)),
        name="window_attention",
    )(flags, sink.astype(F32), qkv, qkv, qkv, qkv, qkv, qkv, qkv, bias)


BF16_ROWS = 16


def _conv_kernel(cur_ref, prev_ref, next_ref, w_ref, b_ref, o_ref, ext_ref, *, tm, starts, ends):
    row0 = pl.program_id(0) * tm
    is_first = functools.reduce(jnp.logical_or, [row0 == s for s in starts])
    is_last = functools.reduce(jnp.logical_or, [row0 + tm == e for e in ends])
    halo = SUBLANES
    prev = prev_ref[...].astype(F32)[BF16_ROWS - halo:, :]
    nxt = next_ref[...].astype(F32)[:halo, :]
    ext_ref[0:halo, :] = jnp.where(is_first, 0.0, prev)
    ext_ref[halo:halo + tm, :] = cur_ref[...].astype(F32)
    ext_ref[halo + tm:2 * halo + tm, :] = jnp.where(is_last, 0.0, nxt)
    half = CONV_WIDTH // 2
    w = w_ref[...]
    acc = jnp.broadcast_to(b_ref[...], o_ref.shape).astype(F32)
    ext = ext_ref[...]
    n = ext.shape[0]
    for k in range(CONV_WIDTH):
        off = halo - half + k
        shift = off % SUBLANES
        src = ext if shift == 0 else pltpu.roll(ext, n - shift, axis=0)
        acc = acc + w[k:k + 1, :] * src[off - shift:off - shift + tm, :]
    o_ref[...] = jax.nn.silu(acc).astype(o_ref.dtype)


def _conv_silu(zxbc, conv_w, conv_b, layer, col0, ncols, tm, tc):
    t = zxbc.shape[0]
    starts, ends = [], []
    s = 0
    for ln in _seq_lens():
        starts.append(s)
        ends.append(s + ln)
        s += ln
    cb0 = col0 // tc
    hb = tm // BF16_ROWS
    last_hb = t // BF16_ROWS - 1
    kern = functools.partial(_conv_kernel, tm=tm, starts=tuple(starts), ends=tuple(ends))
    return pl.pallas_call(
        kern,
        out_shape=jax.ShapeDtypeStruct((t, ncols), BF16),
        grid=(t // tm, ncols // tc),
        in_specs=[
            pl.BlockSpec((tm, tc), lambda i, j: (i, cb0 + j)),
            pl.BlockSpec((BF16_ROWS, tc), lambda i, j: (jnp.maximum(i * hb - 1, 0), cb0 + j)),
            pl.BlockSpec((BF16_ROWS, tc), lambda i, j: (jnp.minimum((i + 1) * hb, last_hb), cb0 + j)),
            pl.BlockSpec((None, CONV_WIDTH, tc), lambda i, j: (layer, 0, j)),
            pl.BlockSpec((None, 1, tc), lambda i, j: (layer, 0, j)),
        ],
        out_specs=pl.BlockSpec((tm, tc), lambda i, j: (i, j)),
        scratch_shapes=[pltpu.VMEM((tm + 2 * SUBLANES, tc), F32)],
        compiler_params=_params(("arbitrary", "arbitrary")),
        name="conv_silu",
    )(zxbc, zxbc, zxbc, conv_w, conv_b.reshape(conv_b.shape[0], 1, conv_b.shape[1]))


def _split3(a):
    a1 = a.astype(BF16)
    r = a - a1.astype(F32)
    a2 = r.astype(BF16)
    a3 = (r - a2.astype(F32)).astype(BF16)
    return a1, a2, a3


def _nt(a, b):
    return lax.dot_general(a, b, (((1,), (1,)), ((), ())), preferred_element_type=F32)


def _ssd_chunk_setup(dt_ref, dtb_ref, a_ref, grow_ref, gld_ref, wst_ref, eoff_ref, cd_ref):
    c = SSD_CHUNK
    nh = dtb_ref.shape[1] // 2
    x = dt_ref[...] + dtb_ref[...]
    dt = jnp.maximum(x, 0.0) + jnp.log1p(jnp.exp(-jnp.abs(x)))
    dt_row = dt.T
    a_row = dt_row * a_ref[...]
    ii = lax.broadcasted_iota(jnp.int32, (c, c), 0)
    jj = lax.broadcasted_iota(jnp.int32, (c, c), 1)
    tl = (jj <= ii).astype(BF16)
    tu = (jj >= ii).astype(BF16)
    parts = _split3(a_row)
    p_row = sum(_nt(p, tl) for p in parts)
    r_row = sum(_nt(p, tu) for p in parts)
    hrow = lax.broadcasted_iota(jnp.int32, a_row.shape, 0)
    fwd = hrow < nh
    g_row = jnp.where(fwd, p_row, r_row)
    g_end = jnp.where(fwd, g_row[:, c - 1:c], g_row[:, 0:1])
    grow_ref[...] = g_row
    gld_ref[...] = g_row - jnp.log(dt_row)
    wst_ref[...] = jnp.exp(g_end - g_row) * dt_row
    eoff_ref[...] = jnp.exp(g_row)
    cd_ref[...] = jnp.broadcast_to(jnp.exp(g_end), cd_ref.shape)


def _ssd_kernel(order_ref, reset_ref, dt_ref, dtb_ref, a_ref, x_ref, b_ref, c_ref, *rest, backward,
                hpg, nh):
    if backward:
        (o_ref, grow_ref, gld_ref, wst_ref, eoff_ref, cd_ref, s_ref) = rest
    else:
        (z_ref, yb_ref, dskip_ref, nw_ref, o_ref,
         grow_ref, gld_ref, wst_ref, eoff_ref, cd_ref, s_ref) = rest
    del order_ref
    c = SSD_CHUNK
    p = SSD_HEAD_DIM
    gw = hpg * p
    d = 1 if backward else 0

    _ssd_chunk_setup(dt_ref, dtb_ref, a_ref, grow_ref, gld_ref, wst_ref, eoff_ref, cd_ref)

    @pl.when(reset_ref[pl.program_id(0)] > 0)
    def _():
        s_ref[...] = jnp.zeros(s_ref.shape, F32)

    si = lax.broadcasted_iota(jnp.int32, (c, c), 0)
    li = lax.broadcasted_iota(jnp.int32, (c, c), 1)

    for g in range(SSD_GROUPS):
        cols = slice(g * gw, (g + 1) * gw)
        ncols = slice(g * D_STATE, (g + 1) * D_STATE)

        def rows(ref, dd, g=g):
            r0 = dd * nh + g * hpg
            return ref[r0:r0 + hpg, :]

        x_t = x_ref[:, cols].astype(F32).T
        bmat = b_ref[:, ncols]
        cmat = c_ref[:, ncols]

        state = s_ref[g]
        eoff = rows(eoff_ref, d)
        wst = rows(wst_ref, d)
        cdr = rows(cd_ref, d)
        y_off = _nt(state.astype(BF16), cmat)
        xw, scaled_state, y_parts = [], [], []
        for r in range(hpg):
            sl = slice(r * p, (r + 1) * p)
            y_parts.append(y_off[sl] * eoff[r:r + 1, :])
            xw.append((x_t[sl] * wst[r:r + 1, :]).astype(BF16))
            scaled_state.append(state[sl] * cdr[r:r + 1, :])
        s_ref[g] = jnp.concatenate(scaled_state, axis=0) + jnp.dot(
            jnp.concatenate(xw, axis=0), bmat, preferred_element_type=F32)
        y_t = jnp.concatenate(y_parts, axis=0)

        if backward:
            o_ref[:, cols] = y_t.T.astype(o_ref.dtype)
            continue

        cbt = _nt(bmat, cmat)
        grow = (rows(grow_ref, 0), rows(grow_ref, 1))
        gld = (rows(gld_ref, 0), rows(gld_ref, 1))
        xb_t = x_t.astype(BF16)
        yd = []
        for r in range(hpg):
            decay = None
            for dd in (0, 1):
                g_l = grow[dd][r:r + 1, :]
                g_s = jnp.broadcast_to(gld[dd][r:r + 1, :], (c, c)).T
                causal = (si <= li) if dd == 0 else (si >= li)
                term = jnp.exp(jnp.where(causal, g_l - g_s, -jnp.inf))
                decay = term if decay is None else decay + term
            yd.append(jnp.dot(xb_t[r * p:(r + 1) * p], (cbt * decay).astype(BF16),
                              preferred_element_type=F32))
        y_t = y_t + jnp.concatenate(yd, axis=0) + dskip_ref[cols, :] * x_t
        y = y_t.T + yb_ref[:, cols].astype(F32)
        gated = y * jax.nn.silu(z_ref[:, cols].astype(F32))
        gated = gated * lax.rsqrt(jnp.mean(gated * gated, axis=-1, keepdims=True) + GATED_NORM_EPS)
        o_ref[:, cols] = (gated * nw_ref[:, cols]).astype(o_ref.dtype)


def _ssd_scan(zxbc, xbc, dt_raw, dt_bias, a_log, d_skip, norm_w, layer):
    t = xbc.shape[0]
    d_inner = SSD_EXPAND * D_MODEL
    nh = d_inner // SSD_HEAD_DIM
    hpg = nh // SSD_GROUPS
    gw = hpg * SSD_HEAD_DIM
    c = SSD_CHUNK
    nchunks = t // c
    gn = SSD_GROUPS * D_STATE
    assert D_STATE == c and d_inner % gn == 0

    fwd_order, bwd_order, reset = [], [], []
    s = 0
    for ln in _seq_lens():
        ids = list(range(s // c, (s + ln) // c))
        fwd_order += ids
        bwd_order += ids[::-1]
        reset += [1] + [0] * (len(ids) - 1)
        s += ln
    reset = jnp.asarray(np.asarray(reset, np.int32))

    dtb = dt_bias[layer].astype(F32).reshape(1, 2 * nh)
    a_neg = -jnp.exp(a_log[layer].astype(F32)).reshape(2 * nh, 1)
    a_rep = jnp.broadcast_to(a_neg, (2 * nh, c))
    dskip = jnp.broadcast_to(jnp.repeat(d_skip[layer].astype(F32), SSD_HEAD_DIM)[:, None], (d_inner, c))
    nw = norm_w[layer].reshape(1, d_inner)

    bblk = d_inner // gn
    tab = pltpu.VMEM((2 * nh, c), F32)
    state = pltpu.VMEM((SSD_GROUPS, gw, D_STATE), F32)

    def common_specs():
        return [
            pl.BlockSpec((c, 2 * nh), lambda i, o, r: (o[i], 0)),
            pl.BlockSpec((1, 2 * nh), lambda i, o, r: (0, 0)),
            pl.BlockSpec((2 * nh, c), lambda i, o, r: (0, 0)),
            pl.BlockSpec((c, d_inner), lambda i, o, r: (o[i], 0)),
            pl.BlockSpec((c, gn), lambda i, o, r: (o[i], bblk)),
            pl.BlockSpec((c, gn), lambda i, o, r: (o[i], bblk + 1)),
        ]

    y_bwd = pl.pallas_call(
        functools.partial(_ssd_kernel, backward=True, hpg=hpg, nh=nh),
        out_shape=jax.ShapeDtypeStruct((t, d_inner), BF16),
        grid_spec=pltpu.PrefetchScalarGridSpec(
            num_scalar_prefetch=2,
            grid=(nchunks,),
            in_specs=common_specs(),
            out_specs=pl.BlockSpec((c, d_inner), lambda i, o, r: (o[i], 0)),
            scratch_shapes=[tab, tab, tab, tab, tab, state],
        ),
        compiler_params=_params(("arbitrary",)),
        name="ssd_backward",
    )(jnp.asarray(np.asarray(bwd_order, np.int32)), reset, dt_raw, dtb, a_rep, xbc, xbc, xbc)

    return pl.pallas_call(
        functools.partial(_ssd_kernel, backward=False, hpg=hpg, nh=nh),
        out_shape=jax.ShapeDtypeStruct((t, d_inner), BF16),
        grid_spec=pltpu.PrefetchScalarGridSpec(
            num_scalar_prefetch=2,
            grid=(nchunks,),
            in_specs=common_specs() + [
                pl.BlockSpec((c, d_inner), lambda i, o, r: (o[i], 0)),
                pl.BlockSpec((c, d_inner), lambda i, o, r: (o[i], 0)),
                pl.BlockSpec((d_inner, c), lambda i, o, r: (0, 0)),
                pl.BlockSpec((1, d_inner), lambda i, o, r: (0, 0)),
            ],
            out_specs=pl.BlockSpec((c, d_inner), lambda i, o, r: (o[i], 0)),
            scratch_shapes=[tab, tab, tab, tab, tab, state],
        ),
        compiler_params=_params(("arbitrary",)),
        name="ssd_forward",
    )(jnp.asarray(np.asarray(fwd_order, np.int32)), reset, dt_raw, dtb, a_rep, xbc, xbc, xbc,
      zxbc, y_bwd, dskip, nw)


HI16 = 0xFFFF0000


def _pack_bf16_pair(a, b):
    ua = pltpu.bitcast(a.astype(BF16).astype(F32), jnp.uint32)
    ub = pltpu.bitcast(b.astype(BF16).astype(F32), jnp.uint32)
    return ua | lax.shift_right_logical(ub, jnp.uint32(16))


def _unpack_bf16_pair(w):
    a = pltpu.bitcast(w & jnp.uint32(HI16), F32).astype(BF16)
    b = pltpu.bitcast(lax.shift_left(w, jnp.uint32(16)), F32).astype(BF16)
    return a, b


def _router_kernel(x_ref, nw_ref, wr_ref, xa_ref, *, n_exp):
    x = x_ref[...]
    half = x.shape[1] // 2
    ms = jnp.mean(x * x, axis=-1, keepdims=True)
    hn = (x * lax.rsqrt(ms + NORM_EPS)) * nw_ref[...]
    h1 = hn.astype(BF16)
    h2 = (hn - h1.astype(F32)).astype(BF16)
    w = wr_ref[...]
    w1 = w.astype(BF16)
    w2 = (w - w1.astype(F32)).astype(BF16)
    logits = (jnp.dot(h1, w1, preferred_element_type=F32) + jnp.dot(h1, w2, preferred_element_type=F32)
              + jnp.dot(h2, w1, preferred_element_type=F32))
    lane = lax.broadcasted_iota(jnp.int32, logits.shape, 1)
    logits = jnp.where(lane < n_exp, logits, -jnp.inf)
    m = jnp.max(logits, axis=-1, keepdims=True)
    e = jnp.exp(logits - m)
    xa_ref[:, :half] = _pack_bf16_pair(hn[:, :half], hn[:, half:])
    xa_ref[:, half:] = pltpu.bitcast(e / jnp.sum(e, axis=-1, keepdims=True), jnp.uint32)


def _router(x, nw, wr, layer, tm):
    t, d = x.shape
    n_exp = wr.shape[-1]
    wr = jnp.pad(wr, ((0, 0), (0, 0), (0, LANES - n_exp)))
    return pl.pallas_call(
        functools.partial(_router_kernel, n_exp=n_exp),
        out_shape=jax.ShapeDtypeStruct((t, d // 2 + LANES), jnp.uint32),
        grid=(t // tm,),
        in_specs=[
            pl.BlockSpec((tm, d), lambda i: (i, 0)),
            pl.BlockSpec((None, 1, d), lambda i: (layer, 0, 0)),
            pl.BlockSpec((None, d, LANES), lambda i: (layer, 0, 0)),
        ],
        out_specs=pl.BlockSpec((tm, d // 2 + LANES), lambda i: (i, 0)),
        compiler_params=_params(("arbitrary",)),
        name="router",
    )(x, nw.reshape(nw.shape[0], 1, d), wr)


def _route_kernel(aff_ref, xa_hbm, idx_ref, xe_hbm, bits_ref, sel_ref, cnt_ref, row_ref, idx_smem, sem_s,
                  sem_g, *, cap, jb):
    nt, ne, _ = aff_ref.shape
    n_tok = nt * LANES
    tok0 = pl.program_id(0) * n_tok
    row0 = pl.program_id(0) * (ne * cap)
    bits_ref[...] = pltpu.bitcast(aff_ref[...], jnp.int32)

    def count_ge(cand):
        def body(ti, acc):
            return acc + (bits_ref[ti] >= cand).astype(jnp.int32)
        acc = lax.fori_loop(0, nt, body, jnp.zeros((ne, LANES), jnp.int32))
        return jnp.sum(acc, axis=1, keepdims=True)

    def bisect(b, prefix):
        cand = prefix | lax.shift_left(jnp.int32(1), 30 - b)
        return jnp.where(count_ge(cand) >= cap, cand, prefix)

    thr = lax.fori_loop(0, 31, bisect, jnp.zeros((ne, 1), jnp.int32))

    def count_gt(ti, acc):
        return acc + (bits_ref[ti] > thr).astype(jnp.int32)
    n_gt = jnp.sum(lax.fori_loop(0, nt, count_gt, jnp.zeros((ne, LANES), jnp.int32)), axis=1, keepdims=True)
    need = (cap - n_gt).astype(F32)

    ii = lax.broadcasted_iota(jnp.int32, (LANES, LANES), 0)
    jj = lax.broadcasted_iota(jnp.int32, (LANES, LANES), 1)
    excl = (ii < jj).astype(BF16)

    def tie_body(ti, run):
        b = bits_ref[ti]
        eq = (b == thr)
        rank = run + jnp.dot(eq.astype(BF16), excl, preferred_element_type=F32)
        sel = jnp.logical_or(b > thr, jnp.logical_and(eq, rank < need))
        sel_ref[ti] = sel.astype(F32)
        return run + jnp.sum(eq.astype(F32), axis=1, keepdims=True)
    lax.fori_loop(0, nt, tie_body, jnp.zeros((ne, 1), F32))

    def cnt_body(ti, run):
        s = sel_ref[ti]
        inc = run + jnp.dot(s.astype(BF16), excl, preferred_element_type=F32) + s
        cnt_ref[ti] = inc
        return run + jnp.sum(s, axis=1, keepdims=True)
    lax.fori_loop(0, nt, cnt_body, jnp.zeros((ne, 1), F32))

    def gather_row(e_src, j):
        tok = idx_smem[0, j]
        pltpu.make_async_copy(xa_hbm.at[pl.ds(tok, 1)], xe_hbm.at[pl.ds(row0 + e_src * cap + j, 1)],
                              sem_g).start()

    ones = jnp.ones((SUBLANES, LANES), BF16)
    per_iter = cap // ((cap // jb) * nt)
    assert per_iter * (cap // jb) * nt == cap
    for e in range(ne):
        for jbi, j0 in enumerate(range(0, cap, jb)):
            jcol = (lax.broadcasted_iota(jnp.int32, (jb, LANES), 0) + j0).astype(F32)

            def body(ti, acc, e=e, jcol=jcol, jbi=jbi):
                row = cnt_ref[ti][e:e + 1, :]
                acc = acc + (row <= jcol).astype(F32)
                if e > 0:
                    for k in range(per_iter):
                        gather_row(e - 1, (jbi * nt + ti) * per_iter + k)
                return acc
            acc = lax.fori_loop(0, nt, body, jnp.zeros((jb, LANES), F32))
            tok = _nt(ones, acc.astype(BF16))
            ids = tok[0:1, :].astype(jnp.int32) + tok0
            idx_ref[e:e + 1, j0:j0 + jb] = ids
            row_ref[:, j0:j0 + jb] = ids
        to_smem = pltpu.make_async_copy(row_ref, idx_smem, sem_s)
        to_smem.start()
        to_smem.wait()

    def last(j, carry):
        gather_row(ne - 1, j)
        return carry
    lax.fori_loop(0, cap, last, 0, unroll=DMA_ISSUE_UNROLL)
    pltpu.make_async_copy(xa_hbm.at[pl.ds(0, ne * cap)], xe_hbm.at[pl.ds(row0, ne * cap)], sem_g).wait()


def _route(aff4, xa, cap):
    n_sets, nt, ne, _ = aff4.shape
    jb = min(cap, 256)
    assert xa.shape[0] >= ne * cap
    return pl.pallas_call(
        functools.partial(_route_kernel, cap=cap, jb=jb),
        out_shape=(jax.ShapeDtypeStruct((n_sets, ne, cap), jnp.int32),
                   jax.ShapeDtypeStruct((n_sets * ne * cap, xa.shape[1]), xa.dtype)),
        grid=(n_sets,),
        in_specs=[pl.BlockSpec((None, nt, ne, LANES), lambda s: (s, 0, 0, 0)),
                  pl.BlockSpec(memory_space=pl.ANY)],
        out_specs=(pl.BlockSpec((None, ne, cap), lambda s: (s, 0, 0)),
                   pl.BlockSpec(memory_space=pl.ANY)),
        scratch_shapes=[pltpu.VMEM((nt, ne, LANES), jnp.int32),
                        pltpu.VMEM((nt, ne, LANES), F32),
                        pltpu.VMEM((nt, ne, LANES), F32),
                        pltpu.VMEM((1, cap), jnp.int32),
                        pltpu.SMEM((1, cap), jnp.int32),
                        pltpu.SemaphoreType.DMA, pltpu.SemaphoreType.DMA],
        compiler_params=_params(("arbitrary",)),
        name="route_topk",
    )(aff4, xa)


def _ffn_kernel(xe_hbm, wg_ref, wu_ref, wd_ref, o_ref, xg_ref, xb_ref, gate_ref, h_ref,
                wgu_ref, wdb_ref, sem, *, cap, d, n_exp, nf):
    e = pl.program_id(0)
    st = pl.program_id(1)
    s = pl.program_id(2)
    half = d // 2

    @pl.when(st == 0)
    def _():
        rows = pltpu.make_async_copy(xe_hbm.at[pl.ds((s * n_exp + e) * cap, cap)], xg_ref, sem)
        rows.start()
        rows.wait()
        a, b = _unpack_bf16_pair(xg_ref[:, :half])
        xb_ref[s] = jnp.concatenate([a, b], axis=1)
        aff = pltpu.bitcast(xg_ref[:, half:], F32)
        lane = lax.broadcasted_iota(jnp.int32, aff.shape, 1)
        gate_ref[s] = jnp.sum(jnp.where(lane == e, aff, 0.0), axis=1, keepdims=True)

    @pl.when(jnp.logical_and(st < nf, s == 0))
    def _():
        wgu_ref[...] = jnp.concatenate([wg_ref[...].astype(BF16), wu_ref[...].astype(BF16)], axis=1)

    @pl.when(st < nf)
    def _():
        tf = wg_ref.shape[1]
        gu = jnp.dot(xb_ref[s], wgu_ref[...], preferred_element_type=F32)
        h_ref[s, st] = (jax.nn.silu(gu[:, :tf]) * gu[:, tf:]).astype(BF16)

    @pl.when(jnp.logical_and(st >= nf, s == 0))
    def _():
        wdb_ref[...] = wd_ref[...].astype(BF16)

    @pl.when(st >= nf)
    def _():
        h = jnp.concatenate([h_ref[s, f] for f in range(nf)], axis=1)
        y = jnp.dot(h, wdb_ref[...], preferred_element_type=F32)
        o_ref[...] = y * gate_ref[s]


def _expert_ffn(xe, wg, wu, wd, layer, cap, tf, tn):
    n_sets = len(_set_lens())
    d = D_MODEL
    n_exp = N_EXPERTS
    ff = wg.shape[-1]
    nf = ff // tf

    def f_of(st):
        return jnp.minimum(st, nf - 1)

    def n_of(st):
        return jnp.maximum(st - nf, 0)

    return pl.pallas_call(
        functools.partial(_ffn_kernel, cap=cap, d=d, n_exp=n_exp, nf=nf),
        out_shape=jax.ShapeDtypeStruct((n_sets, n_exp, cap, d), F32),
        grid=(n_exp, nf + d // tn, n_sets),
        in_specs=[
            pl.BlockSpec(memory_space=pl.ANY),
            pl.BlockSpec((None, None, d, tf), lambda e, st, s: (layer, e, 0, f_of(st))),
            pl.BlockSpec((None, None, d, tf), lambda e, st, s: (layer, e, 0, f_of(st))),
            pl.BlockSpec((None, None, ff, tn), lambda e, st, s: (layer, e, 0, n_of(st))),
        ],
        out_specs=pl.BlockSpec((None, None, cap, tn),
                               lambda e, st, s: (jnp.where(st < nf, 0, s), e, 0, n_of(st))),
        scratch_shapes=[pltpu.VMEM((cap, d // 2 + LANES), jnp.uint32),
                        pltpu.VMEM((n_sets, cap, d), BF16),
                        pltpu.VMEM((n_sets, cap, 1), F32),
                        pltpu.VMEM((n_sets, nf, cap, tf), BF16),
                        pltpu.VMEM((d, 2 * tf), BF16),
                        pltpu.VMEM((ff, tn), BF16),
                        pltpu.SemaphoreType.DMA],
        compiler_params=_params(("arbitrary", "arbitrary", "arbitrary")),
        name="expert_ffn",
    )(xe, wg, wu, wd)


def _combine_kernel(idx_ref, ye_ref, x_in, x_hbm, buf_ref, sem_in, sem_out, *, cap, n_exp):
    del x_in
    s = pl.program_id(0)
    e = pl.program_id(1)
    base = (s * n_exp + e) * cap

    hc = cap // 2
    halves = ((0, sem_in.at[0], sem_out.at[0]), (hc, sem_in.at[1], sem_out.at[1]))

    def gather(j0, sem):
        def body(j, carry):
            tok = idx_ref[base + j]
            pltpu.make_async_copy(x_hbm.at[pl.ds(tok, 1)], buf_ref.at[pl.ds(j, 1)], sem).start()
            return carry
        lax.fori_loop(j0, j0 + hc, body, 0, unroll=DMA_ISSUE_UNROLL)

    def scatter(j0, sem):
        def body(j, carry):
            tok = idx_ref[base + j]
            pltpu.make_async_copy(buf_ref.at[pl.ds(j, 1)], x_hbm.at[pl.ds(tok, 1)], sem).start()
            return carry
        lax.fori_loop(j0, j0 + hc, body, 0, unroll=DMA_ISSUE_UNROLL)

    for j0, si, _ in halves:
        gather(j0, si)
    for j0, si, so in halves:
        rows = pl.ds(j0, hc)
        pltpu.make_async_copy(x_hbm.at[pl.ds(0, hc)], buf_ref.at[rows], si).wait()
        buf_ref[rows, :] = buf_ref[rows, :] + ye_ref[rows, :]
        scatter(j0, so)
    for j0, _, so in halves:
        pltpu.make_async_copy(buf_ref.at[pl.ds(j0, hc)], x_hbm.at[pl.ds(0, hc)], so).wait()


def _combine(idx, ye, x, cap):
    n_sets, n_exp = ye.shape[:2]
    d = x.shape[1]
    return pl.pallas_call(
        functools.partial(_combine_kernel, cap=cap, n_exp=n_exp),
        out_shape=jax.ShapeDtypeStruct(x.shape, x.dtype),
        grid_spec=pltpu.PrefetchScalarGridSpec(
            num_scalar_prefetch=1,
            grid=(n_sets, n_exp),
            in_specs=[
                pl.BlockSpec((None, None, cap, d), lambda s, e, idx: (s, e, 0, 0)),
                pl.BlockSpec(memory_space=pl.ANY),
            ],
            out_specs=pl.BlockSpec(memory_space=pl.ANY),
            scratch_shapes=[pltpu.VMEM((cap, d), F32), pltpu.SemaphoreType.DMA((2,)),
                            pltpu.SemaphoreType.DMA((2,))],
        ),
        input_output_aliases={2: 0},
        compiler_params=_params(("arbitrary", "arbitrary")),
        name="expert_combine",
    )(idx, ye, x)


def _expert_choice_ffn(x, ffn_norm_w, router_w, wg, wu, wd, layer):
    t, d = x.shape
    sets = _set_lens()
    n = sets[0]
    assert all(m == n for m in sets)
    cap = CAPACITY_FACTOR * n // N_EXPERTS
    xa = _router(x, ffn_norm_w, router_w, layer, tm=min(512, t))
    aff = lax.bitcast_convert_type(xa[:, d // 2:d // 2 + N_EXPERTS], F32)
    aff4 = aff.reshape(len(sets), n // LANES, LANES, N_EXPERTS).transpose(0, 1, 3, 2)
    idx, xe = _route(aff4, xa, cap)
    ye = _expert_ffn(xe, wg, wu, wd, layer, cap, tf=_tile(wg.shape[-1], (256, 128)), tn=_tile(d, (256, 128)))
    return _combine(idx.reshape(-1), ye, x, cap)


def _final_norm_kernel(x_ref, nw_ref, o_ref):
    x = x_ref[...]
    ms = jnp.mean(x * x, axis=-1, keepdims=True)
    o_ref[...] = (x * lax.rsqrt(ms + NORM_EPS)) * nw_ref[...]


def _final_norm(x, nw, row0, nrows, tm):
    d = x.shape[1]
    rb0 = row0 // tm
    return pl.pallas_call(
        _final_norm_kernel,
        out_shape=jax.ShapeDtypeStruct((nrows, d), F32),
        grid=(nrows // tm,),
        in_specs=[pl.BlockSpec((tm, d), lambda i: (rb0 + i, 0)),
                  pl.BlockSpec((1, d), lambda i: (0, 0))],
        out_specs=pl.BlockSpec((tm, d), lambda i: (i, 0)),
        compiler_params=_params(("arbitrary",)),
        name="final_norm",
    )(x, nw.reshape(1, d))


def kernel(x_prompt, x_sample, rel_bias, mixer_norm_w, ffn_norm_w, final_norm_w, attn_w_qkv, attn_w_o, attn_sink, ssd_w_in, ssd_conv_w, ssd_conv_b, ssd_dt_bias, ssd_A_log, ssd_D, ssd_norm_w, ssd_w_out, router_w, expert_w_gate, expert_w_up, expert_w_down):
    d = D_MODEL
    x = jnp.concatenate([x_prompt.reshape(-1, d), x_sample.reshape(-1, d)], axis=0)
    t = x.shape[0]
    n_heads = d // HEAD_DIM
    q_dim = n_heads * HEAD_DIM
    qkv_dim = q_dim + 2 * (n_heads // KV_GROUP) * HEAD_DIM
    d_inner = SSD_EXPAND * d
    nh = d_inner // SSD_HEAD_DIM
    conv_dim = d_inner + 2 * SSD_GROUPS * D_STATE
    tm = min(1024, t)
    bias = _t5_bias_table(rel_bias)

    for i in range(DEPTH):
        j = i // N_MIXERS
        hn = _rmsnorm(x, mixer_norm_w, i, tm)
        if i % N_MIXERS == 0:
            qkv = _matmul(hn, attn_w_qkv, j, 0, qkv_dim, _tile(qkv_dim, WIDE_TILES), BF16, tm)
            a = _attention(qkv, bias, attn_sink[j])
            x = _matmul(a, attn_w_o, j, 0, d, _tile(d), F32, tm, x=x)
        else:
            zx_dim = d_inner + conv_dim
            zxbc = _matmul(hn, ssd_w_in, j, 0, zx_dim, _tile(zx_dim, WIDE_TILES), BF16, tm)
            dt_raw = _matmul(hn, ssd_w_in, j, zx_dim, 2 * nh, 2 * nh, F32, tm)
            xbc = _conv_silu(zxbc, ssd_conv_w, ssd_conv_b, j, d_inner, conv_dim, min(512, SEQ),
                             _tile(math.gcd(d_inner, conv_dim)))
            y = _ssd_scan(zxbc, xbc, dt_raw, ssd_dt_bias, ssd_A_log, ssd_D, ssd_norm_w, j)
            x = _matmul(y, ssd_w_out, j, 0, d, _tile(d), F32, tm, x=x)
        x = _expert_choice_ffn(x, ffn_norm_w, router_w, expert_w_gate, expert_w_up, expert_w_down, i)

    n_prompt = BATCH * SEQ
    n_sample = DEC_BATCH * DEC_SEQ
    y_prompt = _final_norm(x, final_norm_w, 0, n_prompt, min(1024, n_prompt))
    y_sample = _final_norm(x, final_norm_w, n_prompt, n_sample, min(1024, n_prompt))
    return (y_prompt.reshape(BATCH, SEQ, d), y_sample.reshape(DEC_BATCH, DEC_SEQ, d))
```

```python
import functools
import math

import jax
import jax.numpy as jnp
import numpy as np
from jax import lax
from jax.experimental import pallas as pl
from jax.experimental.pallas import tpu as pltpu

D_MODEL = 2048
BATCH = 2
SEQ = 4096
DEPTH = 4
DEC_BATCH = 1
DEC_SEQ = 8192

N_MIXERS = 2

HEAD_DIM = 128
WINDOW = 128
BLOCK = 128
REL_BUCKETS = 32
REL_MAX_DIST = 128
KV_GROUP = 4

SSD_EXPAND = 2
SSD_HEAD_DIM = 64
SSD_GROUPS = 8
D_STATE = 128
CONV_WIDTH = 7
SSD_CHUNK = 128

N_EXPERTS = 16
FF_MULT = 2
CAPACITY_FACTOR = 2

NORM_EPS = 1e-6
GATED_NORM_EPS = 1e-5
NEG_INF = -1e30

LANES = 128
SUBLANES = 8
VMEM_LIMIT = 60 * 1024 * 1024
DMA_ISSUE_UNROLL = 8

F32 = jnp.float32
BF16 = jnp.bfloat16


def _seq_lens():
    return [SEQ] * BATCH + [DEC_SEQ] * DEC_BATCH


def _set_lens():
    return [BATCH * SEQ, DEC_BATCH * DEC_SEQ]


WIDE_TILES = (1024, 768, 512, 384, 256, 128)


def _tile(n, cands=(512, 384, 256, 128)):
    return next(c for c in cands if n % c == 0)


def _params(sem):
    return pltpu.CompilerParams(dimension_semantics=sem, vmem_limit_bytes=VMEM_LIMIT)


def _rmsnorm_kernel(x_ref, nw_ref, o_ref):
    x = x_ref[...]
    ms = jnp.mean(x * x, axis=-1, keepdims=True)
    o_ref[...] = ((x * lax.rsqrt(ms + NORM_EPS)) * nw_ref[...]).astype(o_ref.dtype)


def _rmsnorm(x, nw, layer, tm):
    t, d = x.shape
    return pl.pallas_call(
        _rmsnorm_kernel,
        out_shape=jax.ShapeDtypeStruct((t, d), BF16),
        grid=(t // tm,),
        in_specs=[pl.BlockSpec((tm, d), lambda i: (i, 0)),
                  pl.BlockSpec((None, 1, d), lambda i: (layer, 0, 0))],
        out_specs=pl.BlockSpec((tm, d), lambda i: (i, 0)),
        compiler_params=_params(("arbitrary",)),
        name="rmsnorm",
    )(x, nw.reshape(nw.shape[0], 1, d))


def _matmul_kernel(a_ref, w_ref, *rest, residual):
    if residual:
        x_ref, o_ref, wb_ref = rest
    else:
        o_ref, wb_ref = rest

    @pl.when(pl.program_id(1) == 0)
    def _():
        wb_ref[...] = w_ref[...].astype(BF16)

    y = jnp.dot(a_ref[...], wb_ref[...], preferred_element_type=F32)
    if residual:
        y = x_ref[...] + y
    o_ref[...] = y.astype(o_ref.dtype)


def _matmul(a, w, wl, col0, ncols, tn, out_dtype, tm, x=None):
    t, k = a.shape
    cb0 = col0 // tn
    in_specs = [
        pl.BlockSpec((tm, k), lambda j, i: (i, 0)),
        pl.BlockSpec((None, k, tn), lambda j, i: (wl, 0, cb0 + j)),
    ]
    args = [a, w]
    if x is not None:
        in_specs.append(pl.BlockSpec((tm, tn), lambda j, i: (i, j)))
        args.append(x)
    return pl.pallas_call(
        functools.partial(_matmul_kernel, residual=x is not None),
        out_shape=jax.ShapeDtypeStruct((t, ncols), out_dtype),
        grid=(ncols // tn, t // tm),
        in_specs=in_specs,
        out_specs=pl.BlockSpec((tm, tn), lambda j, i: (i, j)),
        scratch_shapes=[pltpu.VMEM((k, tn), BF16)],
        compiler_params=_params(("arbitrary", "arbitrary")),
        name="matmul_residual" if x is not None else "matmul",
    )(*args)


def _t5_bias_table(rel_bias):
    n_heads = D_MODEL // HEAD_DIM
    qi = np.arange(BLOCK)[:, None]
    s = np.arange(3 * BLOCK)[None, :]
    rel = s - BLOCK - qi
    nb = REL_BUCKETS // 2
    ret = (rel > 0).astype(np.int32) * nb
    n = np.abs(rel)
    max_exact = nb // 2
    large = max_exact + (np.log(np.maximum(n, 1) / max_exact)
                         / np.log(REL_MAX_DIST / max_exact) * (nb - max_exact)).astype(np.int32)
    large = np.minimum(large, nb - 1)
    buckets = ret + np.where(n < max_exact, n, large)
    onehot = jnp.asarray(np.eye(REL_BUCKETS, dtype=np.float32)[buckets.reshape(-1)])
    bias = jnp.einsum("pb,bh->hp", onehot, rel_bias.astype(F32), precision=lax.Precision.HIGHEST)
    bias = bias.reshape(n_heads, BLOCK, 3 * BLOCK)
    bias = jnp.where(jnp.asarray(np.abs(rel) <= WINDOW)[None], bias, NEG_INF)
    return bias.reshape(n_heads // KV_GROUP, KV_GROUP, BLOCK, 3 * BLOCK)


def _attention_kernel(flags_ref, sink_ref, q_ref, kp_ref, kc_ref, kn_ref, vp_ref, vc_ref, vn_ref,
                      bias_ref, o_ref):
    n = pl.program_id(0)
    has_prev = flags_ref[2 * n] > 0
    has_next = flags_ref[2 * n + 1] > 0
    n_kv = bias_ref.shape[0]
    rows = KV_GROUP * BLOCK
    col = lax.broadcasted_iota(jnp.int32, (rows, 3 * BLOCK), 1)
    in_seq = jnp.logical_and(jnp.logical_or(col >= BLOCK, has_prev),
                             jnp.logical_or(col < 2 * BLOCK, has_next))
    head = lax.broadcasted_iota(jnp.int32, (rows, 1), 0) // BLOCK
    for kv in range(n_kv):
        hs = slice(kv * HEAD_DIM, (kv + 1) * HEAD_DIM)
        k = jnp.concatenate([kp_ref[:, hs], kc_ref[:, hs], kn_ref[:, hs]], axis=0)
        v = jnp.concatenate([vp_ref[:, hs], vc_ref[:, hs], vn_ref[:, hs]], axis=0)
        q0 = kv * KV_GROUP * HEAD_DIM
        qs = jnp.concatenate([q_ref[:, q0 + g * HEAD_DIM:q0 + (g + 1) * HEAD_DIM] for g in range(KV_GROUP)],
                             axis=0)
        s = lax.dot_general(qs, k, (((1,), (1,)), ((), ())), preferred_element_type=F32)
        s = s * (HEAD_DIM ** -0.5) + bias_ref[kv].reshape(rows, 3 * BLOCK)
        s = jnp.where(in_seq, s, NEG_INF)
        sink = jnp.zeros((rows, 1), F32)
        for g in range(KV_GROUP):
            sink = jnp.where(head == g, sink_ref[kv * KV_GROUP + g], sink)
        m = jnp.maximum(jnp.max(s, axis=-1, keepdims=True), sink)
        p = jnp.exp(s - m)
        den = jnp.sum(p, axis=-1, keepdims=True) + jnp.exp(sink - m)
        o = jnp.dot(p.astype(BF16), v, preferred_element_type=F32) / den
        for g in range(KV_GROUP):
            o_ref[:, q0 + g * HEAD_DIM:q0 + (g + 1) * HEAD_DIM] = o[g * BLOCK:(g + 1) * BLOCK].astype(o_ref.dtype)


def _attention(qkv, bias, sink):
    t = qkv.shape[0]
    n_heads = D_MODEL // HEAD_DIM
    n_kv = n_heads // KV_GROUP
    nblk = t // BLOCK
    flags = np.ones((nblk, 2), np.int32)
    start = 0
    for ln in _seq_lens():
        flags[start // BLOCK, 0] = 0
        flags[(start + ln) // BLOCK - 1, 1] = 0
        start += ln
    flags = jnp.asarray(flags.reshape(-1))
    qw = n_heads * HEAD_DIM
    kw = n_kv * HEAD_DIM
    assert qw % kw == 0
    kcol = qw // kw

    def prev(i):
        return jnp.maximum(i - 1, 0)

    def nxt(i):
        return jnp.minimum(i + 1, nblk - 1)

    blk = (BLOCK, kw)
    return pl.pallas_call(
        _attention_kernel,
        out_shape=jax.ShapeDtypeStruct((t, qw), BF16),
        grid_spec=pltpu.PrefetchScalarGridSpec(
            num_scalar_prefetch=1,
            grid=(nblk,),
            in_specs=[
                pl.BlockSpec(memory_space=pltpu.SMEM),
                pl.BlockSpec((BLOCK, qw), lambda i, f: (i, 0)),
                pl.BlockSpec(blk, lambda i, f: (prev(i), kcol)),
                pl.BlockSpec(blk, lambda i, f: (i, kcol)),
                pl.BlockSpec(blk, lambda i, f: (nxt(i), kcol)),
                pl.BlockSpec(blk, lambda i, f: (prev(i), kcol + 1)),
                pl.BlockSpec(blk, lambda i, f: (i, kcol + 1)),
                pl.BlockSpec(blk, lambda i, f: (nxt(i), kcol + 1)),
                pl.BlockSpec((n_kv, KV_GROUP, BLOCK, 3 * BLOCK), lambda i, f: (0, 0, 0, 0)),
            ],
            out_specs=pl.BlockSpec((BLOCK, qw), lambda i, f: (i, 0)),
        ),
        compiler_params=_params(("arbitrary",)),
        name="window_attention",
    )(flags, sink.astype(F32), qkv, qkv, qkv, qkv, qkv, qkv, qkv, bias)


BF16_ROWS = 16


def _conv_kernel(cur_ref, prev_ref, next_ref, w_ref, b_ref, o_ref, ext_ref, *, tm, starts, ends):
    row0 = pl.program_id(0) * tm
    is_first = functools.reduce(jnp.logical_or, [row0 == s for s in starts])
    is_last = functools.reduce(jnp.logical_or, [row0 + tm == e for e in ends])
    halo = SUBLANES
    prev = prev_ref[...].astype(F32)[BF16_ROWS - halo:, :]
    nxt = next_ref[...].astype(F32)[:halo, :]
    ext_ref[0:halo, :] = jnp.where(is_first, 0.0, prev)
    ext_ref[halo:halo + tm, :] = cur_ref[...].astype(F32)
    ext_ref[halo + tm:2 * halo + tm, :] = jnp.where(is_last, 0.0, nxt)
    half = CONV_WIDTH // 2
    w = w_ref[...]
    acc = jnp.broadcast_to(b_ref[...], o_ref.shape).astype(F32)
    ext = ext_ref[...]
    n = ext.shape[0]
    for k in range(CONV_WIDTH):
        off = halo - half + k
        shift = off % SUBLANES
        src = ext if shift == 0 else pltpu.roll(ext, n - shift, axis=0)
        acc = acc + w[k:k + 1, :] * src[off - shift:off - shift + tm, :]
    o_ref[...] = jax.nn.silu(acc).astype(o_ref.dtype)


def _conv_silu(zxbc, conv_w, conv_b, layer, col0, ncols, tm, tc):
    t = zxbc.shape[0]
    starts, ends = [], []
    s = 0
    for ln in _seq_lens():
        starts.append(s)
        ends.append(s + ln)
        s += ln
    cb0 = col0 // tc
    hb = tm // BF16_ROWS
    last_hb = t // BF16_ROWS - 1
    kern = functools.partial(_conv_kernel, tm=tm, starts=tuple(starts), ends=tuple(ends))
    return pl.pallas_call(
        kern,
        out_shape=jax.ShapeDtypeStruct((t, ncols), BF16),
        grid=(t // tm, ncols // tc),
        in_specs=[
            pl.BlockSpec((tm, tc), lambda i, j: (i, cb0 + j)),
            pl.BlockSpec((BF16_ROWS, tc), lambda i, j: (jnp.maximum(i * hb - 1, 0), cb0 + j)),
            pl.BlockSpec((BF16_ROWS, tc), lambda i, j: (jnp.minimum((i + 1) * hb, last_hb), cb0 + j)),
            pl.BlockSpec((None, CONV_WIDTH, tc), lambda i, j: (layer, 0, j)),
            pl.BlockSpec((None, 1, tc), lambda i, j: (layer, 0, j)),
        ],
        out_specs=pl.BlockSpec((tm, tc), lambda i, j: (i, j)),
        scratch_shapes=[pltpu.VMEM((tm + 2 * SUBLANES, tc), F32)],
        compiler_params=_params(("arbitrary", "arbitrary")),
        name="conv_silu",
    )(zxbc, zxbc, zxbc, conv_w, conv_b.reshape(conv_b.shape[0], 1, conv_b.shape[1]))


def _split3(a):
    a1 = a.astype(BF16)
    r = a - a1.astype(F32)
    a2 = r.astype(BF16)
    a3 = (r - a2.astype(F32)).astype(BF16)
    return a1, a2, a3


def _nt(a, b):
    return lax.dot_general(a, b, (((1,), (1,)), ((), ())), preferred_element_type=F32)


def _ssd_chunk_setup(dt_ref, dtb_ref, a_ref, grow_ref, gld_ref, wst_ref, eoff_ref, cd_ref):
    c = SSD_CHUNK
    nh = dtb_ref.shape[1] // 2
    x = dt_ref[...] + dtb_ref[...]
    dt = jnp.maximum(x, 0.0) + jnp.log1p(jnp.exp(-jnp.abs(x)))
    dt_row = dt.T
    a_row = dt_row * a_ref[...]
    ii = lax.broadcasted_iota(jnp.int32, (c, c), 0)
    jj = lax.broadcasted_iota(jnp.int32, (c, c), 1)
    tl = (jj <= ii).astype(BF16)
    tu = (jj >= ii).astype(BF16)
    parts = _split3(a_row)
    p_row = sum(_nt(p, tl) for p in parts)
    r_row = sum(_nt(p, tu) for p in parts)
    hrow = lax.broadcasted_iota(jnp.int32, a_row.shape, 0)
    fwd = hrow < nh
    g_row = jnp.where(fwd, p_row, r_row)
    g_end = jnp.where(fwd, g_row[:, c - 1:c], g_row[:, 0:1])
    grow_ref[...] = g_row
    gld_ref[...] = g_row - jnp.log(dt_row)
    wst_ref[...] = jnp.exp(g_end - g_row) * dt_row
    eoff_ref[...] = jnp.exp(g_row)
    cd_ref[...] = jnp.broadcast_to(jnp.exp(g_end), cd_ref.shape)


def _ssd_kernel(order_ref, reset_ref, dt_ref, dtb_ref, a_ref, x_ref, b_ref, c_ref, *rest, backward,
                hpg, nh):
    if backward:
        (o_ref, grow_ref, gld_ref, wst_ref, eoff_ref, cd_ref, s_ref) = rest
    else:
        (z_ref, yb_ref, dskip_ref, nw_ref, o_ref,
         grow_ref, gld_ref, wst_ref, eoff_ref, cd_ref, s_ref) = rest
    del order_ref
    c = SSD_CHUNK
    p = SSD_HEAD_DIM
    gw = hpg * p
    d = 1 if backward else 0

    _ssd_chunk_setup(dt_ref, dtb_ref, a_ref, grow_ref, gld_ref, wst_ref, eoff_ref, cd_ref)

    @pl.when(reset_ref[pl.program_id(0)] > 0)
    def _():
        s_ref[...] = jnp.zeros(s_ref.shape, F32)

    si = lax.broadcasted_iota(jnp.int32, (c, c), 0)
    li = lax.broadcasted_iota(jnp.int32, (c, c), 1)

    for g in range(SSD_GROUPS):
        cols = slice(g * gw, (g + 1) * gw)
        ncols = slice(g * D_STATE, (g + 1) * D_STATE)

        def rows(ref, dd, g=g):
            r0 = dd * nh + g * hpg
            return ref[r0:r0 + hpg, :]

        x_t = x_ref[:, cols].astype(F32).T
        bmat = b_ref[:, ncols]
        cmat = c_ref[:, ncols]

        state = s_ref[g]
        eoff = rows(eoff_ref, d)
        wst = rows(wst_ref, d)
        cdr = rows(cd_ref, d)
        y_off = _nt(state.astype(BF16), cmat)
        xw, scaled_state, y_parts = [], [], []
        for r in range(hpg):
            sl = slice(r * p, (r + 1) * p)
            y_parts.append(y_off[sl] * eoff[r:r + 1, :])
            xw.append((x_t[sl] * wst[r:r + 1, :]).astype(BF16))
            scaled_state.append(state[sl] * cdr[r:r + 1, :])
        s_ref[g] = jnp.concatenate(scaled_state, axis=0) + jnp.dot(
            jnp.concatenate(xw, axis=0), bmat, preferred_element_type=F32)
        y_t = jnp.concatenate(y_parts, axis=0)

        if backward:
            o_ref[:, cols] = y_t.T.astype(o_ref.dtype)
            continue

        cbt = _nt(bmat, cmat)
        grow = (rows(grow_ref, 0), rows(grow_ref, 1))
        gld = (rows(gld_ref, 0), rows(gld_ref, 1))
        xb_t = x_t.astype(BF16)
        yd = []
        for r in range(hpg):
            decay = None
            for dd in (0, 1):
                g_l = grow[dd][r:r + 1, :]
                g_s = jnp.broadcast_to(gld[dd][r:r + 1, :], (c, c)).T
                causal = (si <= li) if dd == 0 else (si >= li)
                term = jnp.exp(jnp.where(causal, g_l - g_s, -jnp.inf))
                decay = term if decay is None else decay + term
            yd.append(jnp.dot(xb_t[r * p:(r + 1) * p], (cbt * decay).astype(BF16),
                              preferred_element_type=F32))
        y_t = y_t + jnp.concatenate(yd, axis=0) + dskip_ref[cols, :] * x_t
        y = y_t.T + yb_ref[:, cols].astype(F32)
        gated = y * jax.nn.silu(z_ref[:, cols].astype(F32))
        gated = gated * lax.rsqrt(jnp.mean(gated * gated, axis=-1, keepdims=True) + GATED_NORM_EPS)
        o_ref[:, cols] = (gated * nw_ref[:, cols]).astype(o_ref.dtype)


def _ssd_scan(zxbc, xbc, dt_raw, dt_bias, a_log, d_skip, norm_w, layer):
    t = xbc.shape[0]
    d_inner = SSD_EXPAND * D_MODEL
    nh = d_inner // SSD_HEAD_DIM
    hpg = nh // SSD_GROUPS
    gw = hpg * SSD_HEAD_DIM
    c = SSD_CHUNK
    nchunks = t // c
    gn = SSD_GROUPS * D_STATE
    assert D_STATE == c and d_inner % gn == 0

    fwd_order, bwd_order, reset = [], [], []
    s = 0
    for ln in _seq_lens():
        ids = list(range(s // c, (s + ln) // c))
        fwd_order += ids
        bwd_order += ids[::-1]
        reset += [1] + [0] * (len(ids) - 1)
        s += ln
    reset = jnp.asarray(np.asarray(reset, np.int32))

    dtb = dt_bias[layer].astype(F32).reshape(1, 2 * nh)
    a_neg = -jnp.exp(a_log[layer].astype(F32)).reshape(2 * nh, 1)
    a_rep = jnp.broadcast_to(a_neg, (2 * nh, c))
    dskip = jnp.broadcast_to(jnp.repeat(d_skip[layer].astype(F32), SSD_HEAD_DIM)[:, None], (d_inner, c))
    nw = norm_w[layer].reshape(1, d_inner)

    bblk = d_inner // gn
    tab = pltpu.VMEM((2 * nh, c), F32)
    state = pltpu.VMEM((SSD_GROUPS, gw, D_STATE), F32)

    def common_specs():
        return [
            pl.BlockSpec((c, 2 * nh), lambda i, o, r: (o[i], 0)),
            pl.BlockSpec((1, 2 * nh), lambda i, o, r: (0, 0)),
            pl.BlockSpec((2 * nh, c), lambda i, o, r: (0, 0)),
            pl.BlockSpec((c, d_inner), lambda i, o, r: (o[i], 0)),
            pl.BlockSpec((c, gn), lambda i, o, r: (o[i], bblk)),
            pl.BlockSpec((c, gn), lambda i, o, r: (o[i], bblk + 1)),
        ]

    y_bwd = pl.pallas_call(
        functools.partial(_ssd_kernel, backward=True, hpg=hpg, nh=nh),
        out_shape=jax.ShapeDtypeStruct((t, d_inner), BF16),
        grid_spec=pltpu.PrefetchScalarGridSpec(
            num_scalar_prefetch=2,
            grid=(nchunks,),
            in_specs=common_specs(),
            out_specs=pl.BlockSpec((c, d_inner), lambda i, o, r: (o[i], 0)),
            scratch_shapes=[tab, tab, tab, tab, tab, state],
        ),
        compiler_params=_params(("arbitrary",)),
        name="ssd_backward",
    )(jnp.asarray(np.asarray(bwd_order, np.int32)), reset, dt_raw, dtb, a_rep, xbc, xbc, xbc)

    return pl.pallas_call(
        functools.partial(_ssd_kernel, backward=False, hpg=hpg, nh=nh),
        out_shape=jax.ShapeDtypeStruct((t, d_inner), BF16),
        grid_spec=pltpu.PrefetchScalarGridSpec(
            num_scalar_prefetch=2,
            grid=(nchunks,),
            in_specs=common_specs() + [
                pl.BlockSpec((c, d_inner), lambda i, o, r: (o[i], 0)),
                pl.BlockSpec((c, d_inner), lambda i, o, r: (o[i], 0)),
                pl.BlockSpec((d_inner, c), lambda i, o, r: (0, 0)),
                pl.BlockSpec((1, d_inner), lambda i, o, r: (0, 0)),
            ],
            out_specs=pl.BlockSpec((c, d_inner), lambda i, o, r: (o[i], 0)),
            scratch_shapes=[tab, tab, tab, tab, tab, state],
        ),
        compiler_params=_params(("arbitrary",)),
        name="ssd_forward",
    )(jnp.asarray(np.asarray(fwd_order, np.int32)), reset, dt_raw, dtb, a_rep, xbc, xbc, xbc,
      zxbc, y_bwd, dskip, nw)


HI16 = 0xFFFF0000


def _pack_bf16_pair(a, b):
    ua = pltpu.bitcast(a.astype(BF16).astype(F32), jnp.uint32)
    ub = pltpu.bitcast(b.astype(BF16).astype(F32), jnp.uint32)
    return ua | lax.shift_right_logical(ub, jnp.uint32(16))


def _unpack_bf16_pair(w):
    a = pltpu.bitcast(w & jnp.uint32(HI16), F32).astype(BF16)
    b = pltpu.bitcast(lax.shift_left(w, jnp.uint32(16)), F32).astype(BF16)
    return a, b


def _router_kernel(x_ref, nw_ref, wr_ref, xa_ref, *, n_exp):
    x = x_ref[...]
    half = x.shape[1] // 2
    ms = jnp.mean(x * x, axis=-1, keepdims=True)
    hn = (x * lax.rsqrt(ms + NORM_EPS)) * nw_ref[...]
    h1 = hn.astype(BF16)
    h2 = (hn - h1.astype(F32)).astype(BF16)
    w = wr_ref[...]
    w1 = w.astype(BF16)
    w2 = (w - w1.astype(F32)).astype(BF16)
    logits = (jnp.dot(h1, w1, preferred_element_type=F32) + jnp.dot(h1, w2, preferred_element_type=F32)
              + jnp.dot(h2, w1, preferred_element_type=F32))
    lane = lax.broadcasted_iota(jnp.int32, logits.shape, 1)
    logits = jnp.where(lane < n_exp, logits, -jnp.inf)
    m = jnp.max(logits, axis=-1, keepdims=True)
    e = jnp.exp(logits - m)
    xa_ref[:, :half] = _pack_bf16_pair(hn[:, :half], hn[:, half:])
    xa_ref[:, half:] = pltpu.bitcast(e / jnp.sum(e, axis=-1, keepdims=True), jnp.uint32)


def _router(x, nw, wr, layer, tm):
    t, d = x.shape
    n_exp = wr.shape[-1]
    wr = jnp.pad(wr, ((0, 0), (0, 0), (0, LANES - n_exp)))
    return pl.pallas_call(
        functools.partial(_router_kernel, n_exp=n_exp),
        out_shape=jax.ShapeDtypeStruct((t, d // 2 + LANES), jnp.uint32),
        grid=(t // tm,),
        in_specs=[
            pl.BlockSpec((tm, d), lambda i: (i, 0)),
            pl.BlockSpec((None, 1, d), lambda i: (layer, 0, 0)),
            pl.BlockSpec((None, d, LANES), lambda i: (layer, 0, 0)),
        ],
        out_specs=pl.BlockSpec((tm, d // 2 + LANES), lambda i: (i, 0)),
        compiler_params=_params(("arbitrary",)),
        name="router",
    )(x, nw.reshape(nw.shape[0], 1, d), wr)


def _route_kernel(aff_ref, idx_ref, bits_ref, sel_ref, cnt_ref, *, cap, n_tok, tok0, jb):
    nt, ne, _ = aff_ref.shape
    bits_ref[...] = pltpu.bitcast(aff_ref[...], jnp.int32)

    def count_ge(cand):
        def body(ti, acc):
            return acc + (bits_ref[ti] >= cand).astype(jnp.int32)
        acc = lax.fori_loop(0, nt, body, jnp.zeros((ne, LANES), jnp.int32))
        return jnp.sum(acc, axis=1, keepdims=True)

    def bisect(b, prefix):
        cand = prefix | lax.shift_left(jnp.int32(1), 30 - b)
        return jnp.where(count_ge(cand) >= cap, cand, prefix)

    thr = lax.fori_loop(0, 31, bisect, jnp.zeros((ne, 1), jnp.int32))

    def count_gt(ti, acc):
        return acc + (bits_ref[ti] > thr).astype(jnp.int32)
    n_gt = jnp.sum(lax.fori_loop(0, nt, count_gt, jnp.zeros((ne, LANES), jnp.int32)), axis=1, keepdims=True)
    need = (cap - n_gt).astype(F32)

    ii = lax.broadcasted_iota(jnp.int32, (LANES, LANES), 0)
    jj = lax.broadcasted_iota(jnp.int32, (LANES, LANES), 1)
    excl = (ii < jj).astype(BF16)

    def tie_body(ti, run):
        b = bits_ref[ti]
        eq = (b == thr)
        rank = run + jnp.dot(eq.astype(BF16), excl, preferred_element_type=F32)
        sel = jnp.logical_or(b > thr, jnp.logical_and(eq, rank < need))
        sel_ref[ti] = sel.astype(F32)
        return run + jnp.sum(eq.astype(F32), axis=1, keepdims=True)
    lax.fori_loop(0, nt, tie_body, jnp.zeros((ne, 1), F32))

    def cnt_body(ti, run):
        s = sel_ref[ti]
        inc = run + jnp.dot(s.astype(BF16), excl, preferred_element_type=F32) + s
        cnt_ref[ti] = inc
        return run + jnp.sum(s, axis=1, keepdims=True)
    lax.fori_loop(0, nt, cnt_body, jnp.zeros((ne, 1), F32))

    ones = jnp.ones((SUBLANES, LANES), BF16)
    for e in range(ne):
        for j0 in range(0, cap, jb):
            jcol = (lax.broadcasted_iota(jnp.int32, (jb, LANES), 0) + j0).astype(F32)

            def body(ti, acc, e=e, jcol=jcol):
                row = cnt_ref[ti][e:e + 1, :]
                return acc + (row <= jcol).astype(F32)
            acc = lax.fori_loop(0, nt, body, jnp.zeros((jb, LANES), F32))
            tok = _nt(ones, acc.astype(BF16))
            idx_ref[e:e + 1, j0:j0 + jb] = tok[0:1, :].astype(jnp.int32) + tok0


def _route(aff3, cap, tok0):
    nt, ne, _ = aff3.shape
    jb = min(cap, 256)
    return pl.pallas_call(
        functools.partial(_route_kernel, cap=cap, n_tok=nt * LANES, tok0=tok0, jb=jb),
        out_shape=jax.ShapeDtypeStruct((ne, cap), jnp.int32),
        scratch_shapes=[pltpu.VMEM((nt, ne, LANES), jnp.int32),
                        pltpu.VMEM((nt, ne, LANES), F32),
                        pltpu.VMEM((nt, ne, LANES), F32)],
        compiler_params=pltpu.CompilerParams(vmem_limit_bytes=VMEM_LIMIT),
        name="route_topk",
    )(aff3)


def _gather_rows(idx_ref, base, n, src_hbm, dst_ref, sem):
    def issue(j, carry):
        tok = idx_ref[base + j]
        pltpu.make_async_copy(src_hbm.at[pl.ds(tok, 1)], dst_ref.at[pl.ds(j, 1)], sem).start()
        return carry
    lax.fori_loop(0, n, issue, 0, unroll=DMA_ISSUE_UNROLL)
    pltpu.make_async_copy(src_hbm.at[pl.ds(0, n)], dst_ref, sem).wait()


def _ffn_kernel(idx_ref, xa_hbm, wg_ref, wu_ref, wd_ref, o_ref, xg_ref, xb_ref, gate_ref, h_ref,
                wgu_ref, wdb_ref, sem, *, cap, d, n_exp, nf, n_sets):
    e = pl.program_id(0)
    st = pl.program_id(1)
    half = d // 2

    @pl.when(st == 0)
    def _():
        for s in range(n_sets):
            _gather_rows(idx_ref, (s * n_exp + e) * cap, cap, xa_hbm, xg_ref, sem)
            a, b = _unpack_bf16_pair(xg_ref[:, :half])
            xb_ref[s] = jnp.concatenate([a, b], axis=1)
            aff = pltpu.bitcast(xg_ref[:, half:], F32)
            lane = lax.broadcasted_iota(jnp.int32, aff.shape, 1)
            gate_ref[s] = jnp.sum(jnp.where(lane == e, aff, 0.0), axis=1, keepdims=True)

    @pl.when(st < nf)
    def _():
        tf = wg_ref.shape[1]
        wgu_ref[...] = jnp.concatenate([wg_ref[...].astype(BF16), wu_ref[...].astype(BF16)], axis=1)
        for s in range(n_sets):
            gu = jnp.dot(xb_ref[s], wgu_ref[...], preferred_element_type=F32)
            h_ref[s, st] = (jax.nn.silu(gu[:, :tf]) * gu[:, tf:]).astype(BF16)

    @pl.when(st >= nf)
    def _():
        wdb_ref[...] = wd_ref[...].astype(BF16)
        for s in range(n_sets):
            h = jnp.concatenate([h_ref[s, f] for f in range(nf)], axis=1)
            y = jnp.dot(h, wdb_ref[...], preferred_element_type=F32)
            o_ref[s] = y * gate_ref[s]


def _expert_ffn(idx, xa, wg, wu, wd, layer, cap, tf, tn):
    n_sets = len(_set_lens())
    d = D_MODEL
    n_exp = N_EXPERTS
    ff = wg.shape[-1]
    nf = ff // tf

    def f_of(st):
        return jnp.minimum(st, nf - 1)

    def n_of(st):
        return jnp.maximum(st - nf, 0)

    return pl.pallas_call(
        functools.partial(_ffn_kernel, cap=cap, d=d, n_exp=n_exp, nf=nf, n_sets=n_sets),
        out_shape=jax.ShapeDtypeStruct((n_sets, n_exp, cap, d), F32),
        grid_spec=pltpu.PrefetchScalarGridSpec(
            num_scalar_prefetch=1,
            grid=(n_exp, nf + d // tn),
            in_specs=[
                pl.BlockSpec(memory_space=pl.ANY),
                pl.BlockSpec((None, None, d, tf), lambda e, st, idx: (layer, e, 0, f_of(st))),
                pl.BlockSpec((None, None, d, tf), lambda e, st, idx: (layer, e, 0, f_of(st))),
                pl.BlockSpec((None, None, ff, tn), lambda e, st, idx: (layer, e, 0, n_of(st))),
            ],
            out_specs=pl.BlockSpec((n_sets, None, cap, tn), lambda e, st, idx: (0, e, 0, n_of(st))),
            scratch_shapes=[pltpu.VMEM((cap, d // 2 + LANES), jnp.uint32),
                            pltpu.VMEM((n_sets, cap, d), BF16),
                            pltpu.VMEM((n_sets, cap, 1), F32),
                            pltpu.VMEM((n_sets, nf, cap, tf), BF16),
                            pltpu.VMEM((d, 2 * tf), BF16),
                            pltpu.VMEM((ff, tn), BF16),
                            pltpu.SemaphoreType.DMA],
        ),
        compiler_params=_params(("arbitrary", "arbitrary")),
        name="expert_ffn",
    )(idx, xa, wg, wu, wd)


def _combine_kernel(idx_ref, ye_ref, x_in, x_hbm, buf_ref, sem_in, sem_out, *, cap, n_exp):
    del x_in
    s = pl.program_id(0)
    e = pl.program_id(1)
    base = (s * n_exp + e) * cap

    hc = cap // 2
    halves = ((0, sem_in.at[0], sem_out.at[0]), (hc, sem_in.at[1], sem_out.at[1]))

    def gather(j0, sem):
        def body(j, carry):
            tok = idx_ref[base + j]
            pltpu.make_async_copy(x_hbm.at[pl.ds(tok, 1)], buf_ref.at[pl.ds(j, 1)], sem).start()
            return carry
        lax.fori_loop(j0, j0 + hc, body, 0, unroll=DMA_ISSUE_UNROLL)

    def scatter(j0, sem):
        def body(j, carry):
            tok = idx_ref[base + j]
            pltpu.make_async_copy(buf_ref.at[pl.ds(j, 1)], x_hbm.at[pl.ds(tok, 1)], sem).start()
            return carry
        lax.fori_loop(j0, j0 + hc, body, 0, unroll=DMA_ISSUE_UNROLL)

    for j0, si, _ in halves:
        gather(j0, si)
    for j0, si, so in halves:
        rows = pl.ds(j0, hc)
        pltpu.make_async_copy(x_hbm.at[pl.ds(0, hc)], buf_ref.at[rows], si).wait()
        buf_ref[rows, :] = buf_ref[rows, :] + ye_ref[rows, :]
        scatter(j0, so)
    for j0, _, so in halves:
        pltpu.make_async_copy(buf_ref.at[pl.ds(j0, hc)], x_hbm.at[pl.ds(0, hc)], so).wait()


def _combine(idx, ye, x, cap):
    n_sets, n_exp = ye.shape[:2]
    d = x.shape[1]
    return pl.pallas_call(
        functools.partial(_combine_kernel, cap=cap, n_exp=n_exp),
        out_shape=jax.ShapeDtypeStruct(x.shape, x.dtype),
        grid_spec=pltpu.PrefetchScalarGridSpec(
            num_scalar_prefetch=1,
            grid=(n_sets, n_exp),
            in_specs=[
                pl.BlockSpec((None, None, cap, d), lambda s, e, idx: (s, e, 0, 0)),
                pl.BlockSpec(memory_space=pl.ANY),
            ],
            out_specs=pl.BlockSpec(memory_space=pl.ANY),
            scratch_shapes=[pltpu.VMEM((cap, d), F32), pltpu.SemaphoreType.DMA((2,)),
                            pltpu.SemaphoreType.DMA((2,))],
        ),
        input_output_aliases={2: 0},
        compiler_params=_params(("arbitrary", "arbitrary")),
        name="expert_combine",
    )(idx, ye, x)


def _expert_choice_ffn(x, ffn_norm_w, router_w, wg, wu, wd, layer):
    t = x.shape[0]
    xa = _router(x, ffn_norm_w, router_w, layer, tm=min(512, t))
    half = x.shape[1] // 2
    aff = lax.bitcast_convert_type(xa[:, half:half + N_EXPERTS], F32)
    idx = []
    tok0 = 0
    cap = None
    for n in _set_lens():
        assert cap is None or cap == CAPACITY_FACTOR * n // N_EXPERTS
        cap = CAPACITY_FACTOR * n // N_EXPERTS
        a3 = aff[tok0:tok0 + n].reshape(n // LANES, LANES, N_EXPERTS).transpose(0, 2, 1)
        idx.append(_route(a3, cap, tok0))
        tok0 += n
    idx = jnp.stack(idx).reshape(-1)
    ye = _expert_ffn(idx, xa, wg, wu, wd, layer, cap, tf=_tile(wg.shape[-1], (256, 128)),
                     tn=_tile(x.shape[1], (256, 128)))
    return _combine(idx, ye, x, cap)


def _final_norm_kernel(x_ref, nw_ref, o_ref):
    x = x_ref[...]
    ms = jnp.mean(x * x, axis=-1, keepdims=True)
    o_ref[...] = (x * lax.rsqrt(ms + NORM_EPS)) * nw_ref[...]


def _final_norm(x, nw, row0, nrows, tm):
    d = x.shape[1]
    rb0 = row0 // tm
    return pl.pallas_call(
        _final_norm_kernel,
        out_shape=jax.ShapeDtypeStruct((nrows, d), F32),
        grid=(nrows // tm,),
        in_specs=[pl.BlockSpec((tm, d), lambda i: (rb0 + i, 0)),
                  pl.BlockSpec((1, d), lambda i: (0, 0))],
        out_specs=pl.BlockSpec((tm, d), lambda i: (i, 0)),
        compiler_params=_params(("arbitrary",)),
        name="final_norm",
    )(x, nw.reshape(1, d))


def kernel(x_prompt, x_sample, rel_bias, mixer_norm_w, ffn_norm_w, final_norm_w, attn_w_qkv, attn_w_o, attn_sink, ssd_w_in, ssd_conv_w, ssd_conv_b, ssd_dt_bias, ssd_A_log, ssd_D, ssd_norm_w, ssd_w_out, router_w, expert_w_gate, expert_w_up, expert_w_down):
    d = D_MODEL
    x = jnp.concatenate([x_prompt.reshape(-1, d), x_sample.reshape(-1, d)], axis=0)
    t = x.shape[0]
    n_heads = d // HEAD_DIM
    q_dim = n_heads * HEAD_DIM
    qkv_dim = q_dim + 2 * (n_heads // KV_GROUP) * HEAD_DIM
    d_inner = SSD_EXPAND * d
    nh = d_inner // SSD_HEAD_DIM
    conv_dim = d_inner + 2 * SSD_GROUPS * D_STATE
    tm = min(1024, t)
    bias = _t5_bias_table(rel_bias)

    for i in range(DEPTH):
        j = i // N_MIXERS
        hn = _rmsnorm(x, mixer_norm_w, i, tm)
        if i % N_MIXERS == 0:
            qkv = _matmul(hn, attn_w_qkv, j, 0, qkv_dim, _tile(qkv_dim, WIDE_TILES), BF16, tm)
            a = _attention(qkv, bias, attn_sink[j])
            x = _matmul(a, attn_w_o, j, 0, d, _tile(d), F32, tm, x=x)
        else:
            zx_dim = d_inner + conv_dim
            zxbc = _matmul(hn, ssd_w_in, j, 0, zx_dim, _tile(zx_dim, WIDE_TILES), BF16, tm)
            dt_raw = _matmul(hn, ssd_w_in, j, zx_dim, 2 * nh, 2 * nh, F32, tm)
            xbc = _conv_silu(zxbc, ssd_conv_w, ssd_conv_b, j, d_inner, conv_dim, min(1024, SEQ),
                             _tile(math.gcd(d_inner, conv_dim)))
            y = _ssd_scan(zxbc, xbc, dt_raw, ssd_dt_bias, ssd_A_log, ssd_D, ssd_norm_w, j)
            x = _matmul(y, ssd_w_out, j, 0, d, _tile(d), F32, tm, x=x)
        x = _expert_choice_ffn(x, ffn_norm_w, router_w, expert_w_gate, expert_w_up, expert_w_down, i)

    n_prompt = BATCH * SEQ
    n_sample = DEC_BATCH * DEC_SEQ
    y_prompt = _final_norm(x, final_norm_w, 0, n_prompt, min(1024, n_prompt))
    y_sample = _final_norm(x, final_norm_w, n_prompt, n_sample, min(1024, n_prompt))
    return (y_prompt.reshape(BATCH, SEQ, d), y_sample.reshape(DEC_BATCH, DEC_SEQ, d))
```
